```python
import jax, jax.numpy as jnp
from jax import lax
import numpy as np

D_MODEL = 1024
BATCH = 8
SEQ = 4096
DEPTH = 1

CHUNK = 64
EPS = 1e-6
N_HEADS_A = 8
HEAD_DIM_A = 64
D_A = N_HEADS_A * HEAD_DIM_A
N_PREV_CHUNKS = 8
BAND = (N_PREV_CHUNKS + 1) * CHUNK
REL_CLIP = 128
N_REL = 2 * REL_CLIP + 1
SGU_CHUNK = 128
N_GROUPS_B = 4
GROUP_DIM_B = 128
D_B = N_GROUPS_B * GROUP_DIM_B
SPLITS = (D_A, D_A, D_A, D_A, D_B, D_B, D_B, D_MODEL, D_MODEL)
D_IN = sum(SPLITS)
NEG_INF = -1e30

kernel_name = 'hybrid_chunked_attn_gmlp_gated'


def rmsnorm(x, g):
    xf = x.astype(jnp.float32)
    y = xf * lax.rsqrt(jnp.mean(xf * xf, axis=-1, keepdims=True) + EPS)
    return (y * g.astype(jnp.float32)).astype(x.dtype)


def layernorm(x, g, b):
    xf = x.astype(jnp.float32)
    mu = jnp.mean(xf, axis=-1, keepdims=True)
    var = jnp.mean(jnp.square(xf - mu), axis=-1, keepdims=True)
    y = (xf - mu) * lax.rsqrt(var + EPS)
    return (y * g.astype(jnp.float32) + b.astype(jnp.float32)).astype(x.dtype)


def chunked_rel_attention(q, k, v, rel_bias):
    b, s, _ = q.shape
    nc = s // CHUNK
    qc = q.reshape(b, nc, CHUNK, N_HEADS_A, HEAD_DIM_A)

    def band(t):
        t = t.reshape(b, nc, CHUNK, N_HEADS_A, HEAD_DIM_A)
        tp = jnp.pad(t, ((0, 0), (N_PREV_CHUNKS, 0), (0, 0), (0, 0), (0, 0)))
        return jnp.concatenate([tp[:, j:j + nc] for j in range(N_PREV_CHUNKS + 1)], axis=2)

    kb, vb = band(k), band(v)
    q_off = jnp.arange(CHUNK)
    k_off = jnp.arange(BAND) - N_PREV_CHUNKS * CHUNK
    dist = q_off[:, None] - k_off[None, :]
    bias = rel_bias[:, jnp.clip(dist, -REL_CLIP, REL_CLIP) + REL_CLIP].astype(jnp.float32)
    key_chunk = jnp.arange(nc)[:, None] + k_off[None, :] // CHUNK
    valid = key_chunk >= 0
    scale = HEAD_DIM_A ** -0.5
    scores = jnp.einsum('bnqhd,bnkhd->bhnqk', qc, kb).astype(jnp.float32) * scale
    scores = scores + bias[None, :, None, :, :]
    scores = jnp.where(valid[None, None, :, None, :], scores, NEG_INF)
    p = jax.nn.softmax(scores, axis=-1).astype(v.dtype)
    out = jnp.einsum('bhnqk,bnkhd->bnqhd', p, vb)
    return out.reshape(b, s, D_A)


def spatial_gating(u, v, ln_g, ln_b, w_s, b_s):
    b, s, _ = v.shape
    nb = s // SGU_CHUNK
    vn = layernorm(v, ln_g, ln_b).reshape(b, nb, SGU_CHUNK, N_GROUPS_B, GROUP_DIM_B)
    tri = jnp.tril(jnp.ones((SGU_CHUNK, SGU_CHUNK), dtype=bool))
    w = jnp.where(tri[None], w_s, jnp.zeros_like(w_s))
    mixed = jnp.einsum('gts,bnsgc->bntgc', w, vn) + jnp.transpose(b_s)[:, :, None]
    return u * mixed.reshape(b, s, D_B)


def hybrid_layer(x, norm_g, w_in, b_gate, rel_bias, sgu_ln_g, sgu_ln_b, w_s, b_s, w_pa, w_pb, w_out):
    h = rmsnorm(x, norm_g)
    z = jnp.einsum('bsd,de->bse', h, w_in)
    idx = list(np.cumsum(SPLITS)[:-1])
    q, k, v, g_a, u_b, v_b, g_b, gate_a, gate_b = jnp.split(z, idx, axis=-1)
    y_a = chunked_rel_attention(q, k, v, rel_bias) * jax.nn.silu(g_a)
    y_b = spatial_gating(jax.nn.gelu(u_b), jax.nn.gelu(v_b), sgu_ln_g, sgu_ln_b, w_s, b_s) * jax.nn.silu(g_b)
    p_a = jnp.einsum('bse,ed->bsd', y_a, w_pa)
    p_b = jnp.einsum('bse,ed->bsd', y_b, w_pb)
    ga = jax.nn.sigmoid(gate_a + b_gate[:D_MODEL])
    gb = jax.nn.sigmoid(gate_b + b_gate[D_MODEL:])
    merged = ga * p_a + gb * p_b
    return x + jnp.einsum('bsd,de->bse', merged, w_out)


def setup_inputs(seed: int = 0) -> dict:
    key = jax.random.key(seed)
    ks = jax.random.split(key, 16)
    f32 = jnp.float32
    nrm = lambda k, shape, s: jax.random.normal(k, shape, f32) * s
    return {
        'x': jax.random.normal(ks[0], (BATCH, SEQ, D_MODEL), f32),
        'norm_g': 1.0 + nrm(ks[1], (DEPTH, D_MODEL), 0.05),
        'w_in': nrm(ks[2], (DEPTH, D_MODEL, D_IN), D_MODEL ** -0.5),
        'b_gate': nrm(ks[3], (DEPTH, 2 * D_MODEL), 0.1),
        'rel_bias': nrm(ks[4], (DEPTH, N_HEADS_A, N_REL), 0.5),
        'sgu_ln_g': 1.0 + nrm(ks[5], (DEPTH, D_B), 0.05),
        'sgu_ln_b': nrm(ks[6], (DEPTH, D_B), 0.05),
        'w_s': nrm(ks[7], (DEPTH, N_GROUPS_B, SGU_CHUNK, SGU_CHUNK), SGU_CHUNK ** -0.5),
        'b_s': 1.0 + nrm(ks[8], (DEPTH, N_GROUPS_B, SGU_CHUNK), 0.1),
        'w_pa': nrm(ks[9], (DEPTH, D_A, D_MODEL), D_A ** -0.5),
        'w_pb': nrm(ks[10], (DEPTH, D_B, D_MODEL), D_B ** -0.5),
        'w_out': nrm(ks[11], (DEPTH, D_MODEL, D_MODEL), D_MODEL ** -0.5),
        'final_g': 1.0 + nrm(ks[12], (D_MODEL,), 0.05),
    }


def reference(x, norm_g, w_in, b_gate, rel_bias, sgu_ln_g, sgu_ln_b, w_s, b_s, w_pa, w_pb, w_out, final_g):
    for l in range(DEPTH):
        x = hybrid_layer(x, norm_g[l], w_in[l], b_gate[l], rel_bias[l], sgu_ln_g[l], sgu_ln_b[l],
                         w_s[l], b_s[l], w_pa[l], w_pb[l], w_out[l])
    return rmsnorm(x, final_g)
```

```python
import functools

import jax
import jax.numpy as jnp
from jax import lax
from jax.experimental import pallas as pl
from jax.experimental.pallas import tpu as pltpu

D_MODEL = 1024
CHUNK = 64
EPS = 1e-6
N_HEADS_A = 8
HEAD_DIM_A = 64
D_A = N_HEADS_A * HEAD_DIM_A
N_PREV_CHUNKS = 8
REL_CLIP = 128
SGU_CHUNK = 128
N_GROUPS_B = 4
GROUP_DIM_B = 128
D_B = N_GROUPS_B * GROUP_DIM_B
NEG_INF = -1e30

LANES = 128
HEADS_PER_SLAB = LANES // HEAD_DIM_A
N_SLABS = N_HEADS_A // HEADS_PER_SLAB

TILE = 512
Q_GROUP_CHUNKS = 4
Q_GROUP = Q_GROUP_CHUNKS * CHUNK
KEY_WINDOW = (Q_GROUP_CHUNKS + N_PREV_CHUNKS) * CHUNK
PREV = N_PREV_CHUNKS * CHUNK
N_Q_GROUPS = TILE // Q_GROUP

OFF_Q = 0
OFF_K = OFF_Q + D_A
OFF_V = OFF_K + D_A
OFF_GA = OFF_V + D_A
OFF_UB = OFF_GA + D_A
OFF_VB = OFF_UB + D_B
OFF_GB = OFF_VB + D_B
OFF_GATE_A = OFF_GB + D_B
OFF_GATE_B = OFF_GATE_A + D_MODEL
D_IN = OFF_GATE_B + D_MODEL

VMEM_LIMIT_BYTES = 60 * 1024 * 1024

_NT = (((1,), (1,)), ((), ()))


def _proj(h, w_ref, off, width):
    return jnp.dot(h, w_ref[:, off:off + width], preferred_element_type=jnp.float32)


def _block_kernel(x_ref, ng_ref, win_ref, bgate_ref, bias_ref, lng_ref, lnb_ref, ws_ref,
                  bs_ref, wpa_ref, wpb_ref, wout_ref, fg_ref, o_ref,
                  h_ref, q_ref, k_ref, v_ref, ya_ref):
    t = pl.program_id(1)
    bf16 = jnp.bfloat16
    f32 = jnp.float32

    @pl.when(t == 0)
    def _():
        k_ref[:PREV, :] = jnp.zeros((PREV, D_A), bf16)
        v_ref[:PREV, :] = jnp.zeros((PREV, D_A), bf16)

    x = x_ref[0]
    inv = lax.rsqrt(jnp.mean(x * x, axis=-1, keepdims=True) + EPS)
    h_ref[...] = (x * inv * ng_ref[...]).astype(bf16)
    h = h_ref[...]

    q_ref[...] = (_proj(h, win_ref, OFF_Q, D_A) * (HEAD_DIM_A ** -0.5)).astype(bf16)
    k_ref[PREV:, :] = _proj(h, win_ref, OFF_K, D_A).astype(bf16)
    v_ref[PREV:, :] = _proj(h, win_ref, OFF_V, D_A).astype(bf16)
    gate_path_a = jax.nn.silu(_proj(h, win_ref, OFF_GA, D_A))

    lane = lax.broadcasted_iota(jnp.int32, (Q_GROUP, LANES), 1)
    first_head = lane < HEAD_DIM_A
    col = lax.broadcasted_iota(jnp.int32, (1, KEY_WINDOW), 1)
    for g in range(N_Q_GROUPS):
        r0 = g * Q_GROUP
        pen = jnp.where((t == 0) & (col < PREV - r0), NEG_INF, 0.0).astype(f32)
        for s in range(N_SLABS):
            c0 = s * LANES
            q2 = q_ref[r0:r0 + Q_GROUP, c0:c0 + LANES]
            zero = jnp.zeros_like(q2)
            qm = jnp.concatenate([jnp.where(first_head, q2, zero),
                                  jnp.where(first_head, zero, q2)], axis=0)
            ks = k_ref[r0:r0 + KEY_WINDOW, c0:c0 + LANES]
            vs = v_ref[r0:r0 + KEY_WINDOW, c0:c0 + LANES]
            sc = lax.dot_general(qm, ks, _NT, preferred_element_type=f32)
            bias = bias_ref[HEADS_PER_SLAB * s:HEADS_PER_SLAB * (s + 1)]
            sc = sc + bias.reshape(HEADS_PER_SLAB * Q_GROUP, KEY_WINDOW) + pen
            m = jnp.max(sc, axis=-1, keepdims=True)
            e = jnp.exp(sc - m)
            l = jnp.sum(e, axis=-1, keepdims=True)
            o = jnp.dot(e.astype(bf16), vs, preferred_element_type=f32) / l
            o2 = jnp.where(first_head, o[:Q_GROUP], o[Q_GROUP:])
            ya_ref[r0:r0 + Q_GROUP, c0:c0 + LANES] = (
                o2 * gate_path_a[r0:r0 + Q_GROUP, c0:c0 + LANES]).astype(bf16)

    k_ref[:PREV, :] = k_ref[PREV:, :]
    v_ref[:PREV, :] = v_ref[PREV:, :]

    p_a = jnp.dot(ya_ref[...], wpa_ref[...], preferred_element_type=f32)

    u = jax.nn.gelu(_proj(h, win_ref, OFF_UB, D_B))
    vb = jax.nn.gelu(_proj(h, win_ref, OFF_VB, D_B))
    mu = jnp.mean(vb, axis=-1, keepdims=True)
    var = jnp.mean(jnp.square(vb - mu), axis=-1, keepdims=True)
    vn = ((vb - mu) * lax.rsqrt(var + EPS) * lng_ref[...] + lnb_ref[...]).astype(bf16)
    gate_path_b = jax.nn.silu(_proj(h, win_ref, OFF_GB, D_B))
    row = lax.broadcasted_iota(jnp.int32, (SGU_CHUNK, SGU_CHUNK), 0)
    colq = lax.broadcasted_iota(jnp.int32, (SGU_CHUNK, SGU_CHUNK), 1)
    causal = row >= colq
    mixed_cols = []
    for gi in range(N_GROUPS_B):
        w_g = jnp.where(causal, ws_ref[gi], 0.0).astype(bf16)
        c0 = gi * GROUP_DIM_B
        rows = []
        for n in range(TILE // SGU_CHUNK):
            r0 = n * SGU_CHUNK
            rows.append(jnp.dot(w_g, vn[r0:r0 + SGU_CHUNK, c0:c0 + GROUP_DIM_B],
                                preferred_element_type=f32) + bs_ref[gi])
        mixed_cols.append(jnp.concatenate(rows, axis=0))
    mixed = jnp.concatenate(mixed_cols, axis=1)
    y_b = (u * mixed * gate_path_b).astype(bf16)
    p_b = jnp.dot(y_b, wpb_ref[...], preferred_element_type=f32)

    ga = jax.nn.sigmoid(_proj(h, win_ref, OFF_GATE_A, D_MODEL) + bgate_ref[:, :D_MODEL])
    gb = jax.nn.sigmoid(_proj(h, win_ref, OFF_GATE_B, D_MODEL) + bgate_ref[:, D_MODEL:])
    merged = (ga * p_a + gb * p_b).astype(bf16)
    y = x_ref[0] + jnp.dot(merged, wout_ref[...], preferred_element_type=f32)
    inv_y = lax.rsqrt(jnp.mean(y * y, axis=-1, keepdims=True) + EPS)
    o_ref[0] = y * inv_y * fg_ref[...]


def _bias_table(rel_bias):
    r = jnp.arange(Q_GROUP)[:, None]
    c = jnp.arange(KEY_WINDOW)[None, :]
    dist = r - (c - PREV)
    band = (c // CHUNK >= r // CHUNK) & (c // CHUNK <= r // CHUNK + N_PREV_CHUNKS)
    tbl = rel_bias[:, jnp.clip(dist, -REL_CLIP, REL_CLIP) + REL_CLIP].astype(jnp.float32)
    return jnp.where(band[None], tbl, NEG_INF)


def _const_spec(shape):
    zeros = (0,) * len(shape)
    return pl.BlockSpec(shape, lambda b, t: zeros, pipeline_mode=pl.Buffered(1))


@jax.jit
def kernel(x, norm_g, w_in, b_gate, rel_bias, sgu_ln_g, sgu_ln_b, w_s, b_s, w_pa, w_pb, w_out, final_g):
    batch, seq, d = x.shape
    assert d == D_MODEL and seq % TILE == 0 and norm_g.shape[0] == 1
    bf16 = jnp.bfloat16
    row = lambda a: a.reshape(1, -1).astype(jnp.float32)
    bias_tbl = _bias_table(rel_bias[0])
    bs_b = jnp.broadcast_to(b_s[0][:, :, None], (N_GROUPS_B, SGU_CHUNK, GROUP_DIM_B)).astype(jnp.float32)
    operands = (
        x, row(norm_g[0]), w_in[0].astype(bf16), row(b_gate[0]), bias_tbl,
        row(sgu_ln_g[0]), row(sgu_ln_b[0]), w_s[0], bs_b,
        w_pa[0].astype(bf16), w_pb[0].astype(bf16), w_out[0].astype(bf16), row(final_g),
    )
    x_spec = pl.BlockSpec((1, TILE, D_MODEL), lambda b, t: (b, t, 0))
    in_specs = [x_spec] + [_const_spec(a.shape) for a in operands[1:]]
    return pl.pallas_call(
        _block_kernel,
        grid=(batch, seq // TILE),
        in_specs=in_specs,
        out_specs=pl.BlockSpec((1, TILE, D_MODEL), lambda b, t: (b, t, 0)),
        out_shape=jax.ShapeDtypeStruct(x.shape, x.dtype),
        scratch_shapes=[
            pltpu.VMEM((TILE, D_MODEL), bf16),
            pltpu.VMEM((TILE, D_A), bf16),
            pltpu.VMEM((PREV + TILE, D_A), bf16),
            pltpu.VMEM((PREV + TILE, D_A), bf16),
            pltpu.VMEM((TILE, D_A), bf16),
        ],
        compiler_params=pltpu.CompilerParams(
            dimension_semantics=("arbitrary", "arbitrary"),
            vmem_limit_bytes=VMEM_LIMIT_BYTES,
        ),
        name="hybrid_block",
    )(*operands)
```

```python
import jax
import jax.numpy as jnp
from jax import lax
from jax.experimental import pallas as pl
from jax.experimental.pallas import tpu as pltpu

D_MODEL = 1024
CHUNK = 64
EPS = 1e-6
N_HEADS_A = 8
HEAD_DIM_A = 64
D_A = N_HEADS_A * HEAD_DIM_A
N_PREV_CHUNKS = 8
REL_CLIP = 128
SGU_CHUNK = 128
N_GROUPS_B = 4
GROUP_DIM_B = 128
D_B = N_GROUPS_B * GROUP_DIM_B
NEG_INF = -1e30

LANES = 128
HEADS_PER_SLAB = LANES // HEAD_DIM_A
N_SLABS = N_HEADS_A // HEADS_PER_SLAB

TILE = 512
Q_GROUP_CHUNKS = 4
Q_GROUP = Q_GROUP_CHUNKS * CHUNK
KEY_WINDOW = (Q_GROUP_CHUNKS + N_PREV_CHUNKS) * CHUNK
PREV = N_PREV_CHUNKS * CHUNK
N_Q_GROUPS = TILE // Q_GROUP
REL_RING = Q_GROUP + KEY_WINDOW

OFF_Q = 0
OFF_K = OFF_Q + D_A
OFF_V = OFF_K + D_A
OFF_GA = OFF_V + D_A
OFF_UB = OFF_GA + D_A
OFF_VB = OFF_UB + D_B
OFF_GB = OFF_VB + D_B
OFF_GATE_A = OFF_GB + D_B
OFF_GATE_B = OFF_GATE_A + D_MODEL
D_IN = OFF_GATE_B + D_MODEL

VMEM_LIMIT_BYTES = 60 * 1024 * 1024

_NT = (((1,), (1,)), ((), ()))


def _proj(h, w_ref, off, width):
    return jnp.dot(h, w_ref[:, off:off + width], preferred_element_type=jnp.float32)


def _block_kernel(x_ref, ng_ref, win_ref, bgate_ref, relb_ref, lng_ref, lnb_ref, ws_ref,
                  bs_ref, wpa_ref, wpb_ref, wout_ref, fg_ref, o_ref,
                  h_ref, q_ref, k_ref, v_ref, ya_ref, bias_ref):
    t = pl.program_id(1)
    bf16 = jnp.bfloat16
    f32 = jnp.float32

    @pl.when((pl.program_id(0) == 0) & (t == 0))
    def _():
        r = lax.broadcasted_iota(jnp.int32, (Q_GROUP, KEY_WINDOW), 0) // CHUNK
        c = lax.broadcasted_iota(jnp.int32, (Q_GROUP, KEY_WINDOW), 1) // CHUNK
        band = (c >= r) & (c <= r + N_PREV_CHUNKS)
        for hd in range(N_HEADS_A):
            base = jnp.broadcast_to(relb_ref[hd:hd + 1, :], (Q_GROUP, REL_RING))
            toeplitz = pltpu.roll(base, 0, 1, stride=1, stride_axis=0)
            bias_ref[hd] = jnp.where(band, toeplitz[:, :KEY_WINDOW], NEG_INF)

    @pl.when(t == 0)
    def _():
        k_ref[:PREV, :] = jnp.zeros((PREV, D_A), bf16)
        v_ref[:PREV, :] = jnp.zeros((PREV, D_A), bf16)

    x = x_ref[0]
    inv = lax.rsqrt(jnp.mean(x * x, axis=-1, keepdims=True) + EPS)
    h_ref[...] = (x * inv * ng_ref[...]).astype(bf16)
    h = h_ref[...]

    q_ref[...] = (_proj(h, win_ref, OFF_Q, D_A) * (HEAD_DIM_A ** -0.5)).astype(bf16)
    k_ref[PREV:, :] = _proj(h, win_ref, OFF_K, D_A).astype(bf16)
    v_ref[PREV:, :] = _proj(h, win_ref, OFF_V, D_A).astype(bf16)
    gate_path_a = jax.nn.silu(_proj(h, win_ref, OFF_GA, D_A))

    lane = lax.broadcasted_iota(jnp.int32, (Q_GROUP, LANES), 1)
    first_head = lane < HEAD_DIM_A
    col = lax.broadcasted_iota(jnp.int32, (1, KEY_WINDOW), 1)
    for g in range(N_Q_GROUPS):
        r0 = g * Q_GROUP
        pen = jnp.where((t == 0) & (col < PREV - r0), NEG_INF, 0.0).astype(f32)
        for s in range(N_SLABS):
            c0 = s * LANES
            q2 = q_ref[r0:r0 + Q_GROUP, c0:c0 + LANES]
            zero = jnp.zeros_like(q2)
            qm = jnp.concatenate([jnp.where(first_head, q2, zero),
                                  jnp.where(first_head, zero, q2)], axis=0)
            ks = k_ref[r0:r0 + KEY_WINDOW, c0:c0 + LANES]
            vs = v_ref[r0:r0 + KEY_WINDOW, c0:c0 + LANES]
            sc = lax.dot_general(qm, ks, _NT, preferred_element_type=f32)
            bias = bias_ref[HEADS_PER_SLAB * s:HEADS_PER_SLAB * (s + 1)]
            sc = sc + bias.reshape(HEADS_PER_SLAB * Q_GROUP, KEY_WINDOW) + pen
            m = jnp.max(sc, axis=-1, keepdims=True)
            e = jnp.exp(sc - m)
            l = jnp.sum(e, axis=-1, keepdims=True)
            o = jnp.dot(e.astype(bf16), vs, preferred_element_type=f32) / l
            o2 = jnp.where(first_head, o[:Q_GROUP], o[Q_GROUP:])
            ya_ref[r0:r0 + Q_GROUP, c0:c0 + LANES] = (
                o2 * gate_path_a[r0:r0 + Q_GROUP, c0:c0 + LANES]).astype(bf16)

    k_ref[:PREV, :] = k_ref[PREV:, :]
    v_ref[:PREV, :] = v_ref[PREV:, :]

    p_a = jnp.dot(ya_ref[...], wpa_ref[...], preferred_element_type=f32)

    u = jax.nn.gelu(_proj(h, win_ref, OFF_UB, D_B))
    vb = jax.nn.gelu(_proj(h, win_ref, OFF_VB, D_B))
    mu = jnp.mean(vb, axis=-1, keepdims=True)
    var = jnp.mean(jnp.square(vb - mu), axis=-1, keepdims=True)
    vn = ((vb - mu) * lax.rsqrt(var + EPS) * lng_ref[...] + lnb_ref[...]).astype(bf16)
    gate_path_b = jax.nn.silu(_proj(h, win_ref, OFF_GB, D_B))
    row = lax.broadcasted_iota(jnp.int32, (SGU_CHUNK, SGU_CHUNK), 0)
    colq = lax.broadcasted_iota(jnp.int32, (SGU_CHUNK, SGU_CHUNK), 1)
    causal = row >= colq
    mixed_cols = []
    for gi in range(N_GROUPS_B):
        w_g = jnp.where(causal, ws_ref[gi], 0.0).astype(bf16)
        c0 = gi * GROUP_DIM_B
        rows = []
        for n in range(TILE // SGU_CHUNK):
            r0 = n * SGU_CHUNK
            rows.append(jnp.dot(w_g, vn[r0:r0 + SGU_CHUNK, c0:c0 + GROUP_DIM_B],
                                preferred_element_type=f32) + bs_ref[gi])
        mixed_cols.append(jnp.concatenate(rows, axis=0))
    mixed = jnp.concatenate(mixed_cols, axis=1)
    y_b = (u * mixed * gate_path_b).astype(bf16)
    p_b = jnp.dot(y_b, wpb_ref[...], preferred_element_type=f32)

    ga = jax.nn.sigmoid(_proj(h, win_ref, OFF_GATE_A, D_MODEL) + bgate_ref[:, :D_MODEL])
    gb = jax.nn.sigmoid(_proj(h, win_ref, OFF_GATE_B, D_MODEL) + bgate_ref[:, D_MODEL:])
    merged = (ga * p_a + gb * p_b).astype(bf16)
    y = x_ref[0] + jnp.dot(merged, wout_ref[...], preferred_element_type=f32)
    inv_y = lax.rsqrt(jnp.mean(y * y, axis=-1, keepdims=True) + EPS)
    o_ref[0] = y * inv_y * fg_ref[...]


def _rel_ring(rel_bias):
    heads = rel_bias.shape[0]
    far = lambda n: jnp.broadcast_to(rel_bias[:, -1:], (heads, n))
    near = lambda n: jnp.broadcast_to(rel_bias[:, :1], (heads, n))
    n_far = PREV - REL_CLIP
    n_near = KEY_WINDOW - n_far - rel_bias.shape[1]
    ring = jnp.concatenate([far(n_far), rel_bias[:, ::-1], near(n_near), far(REL_RING - KEY_WINDOW)], axis=1)
    return ring.astype(jnp.float32)


def _const_spec(shape):
    zeros = (0,) * len(shape)
    return pl.BlockSpec(shape, lambda b, t: zeros, pipeline_mode=pl.Buffered(1))


@jax.jit
def kernel(x, norm_g, w_in, b_gate, rel_bias, sgu_ln_g, sgu_ln_b, w_s, b_s, w_pa, w_pb, w_out, final_g):
    batch, seq, d = x.shape
    assert d == D_MODEL and seq % TILE == 0 and norm_g.shape[0] == 1
    bf16 = jnp.bfloat16
    row = lambda a: a.reshape(1, -1).astype(jnp.float32)
    bs_b = jnp.broadcast_to(b_s[0][:, :, None], (N_GROUPS_B, SGU_CHUNK, GROUP_DIM_B)).astype(jnp.float32)
    operands = (
        x, row(norm_g[0]), w_in[0].astype(bf16), row(b_gate[0]), _rel_ring(rel_bias[0]),
        row(sgu_ln_g[0]), row(sgu_ln_b[0]), w_s[0], bs_b,
        w_pa[0].astype(bf16), w_pb[0].astype(bf16), w_out[0].astype(bf16), row(final_g),
    )
    x_spec = pl.BlockSpec((1, TILE, D_MODEL), lambda b, t: (b, t, 0))
    in_specs = [x_spec] + [_const_spec(a.shape) for a in operands[1:]]
    return pl.pallas_call(
        _block_kernel,
        grid=(batch, seq // TILE),
        in_specs=in_specs,
        out_specs=pl.BlockSpec((1, TILE, D_MODEL), lambda b, t: (b, t, 0)),
        out_shape=jax.ShapeDtypeStruct(x.shape, x.dtype),
        scratch_shapes=[
            pltpu.VMEM((TILE, D_MODEL), bf16),
            pltpu.VMEM((TILE, D_A), bf16),
            pltpu.VMEM((PREV + TILE, D_A), bf16),
            pltpu.VMEM((PREV + TILE, D_A), bf16),
            pltpu.VMEM((TILE, D_A), bf16),
            pltpu.VMEM((N_HEADS_A, Q_GROUP, KEY_WINDOW), jnp.float32),
        ],
        compiler_params=pltpu.CompilerParams(
            dimension_semantics=("arbitrary", "arbitrary"),
            vmem_limit_bytes=VMEM_LIMIT_BYTES,
        ),
        name="hybrid_block",
    )(*operands)
```

```python
import jax
import jax.numpy as jnp
from jax import lax
from jax.experimental import pallas as pl
from jax.experimental.pallas import tpu as pltpu

D_MODEL = 1024
CHUNK = 64
EPS = 1e-6
N_HEADS_A = 8
HEAD_DIM_A = 64
D_A = N_HEADS_A * HEAD_DIM_A
N_PREV_CHUNKS = 8
REL_CLIP = 128
SGU_CHUNK = 128
N_GROUPS_B = 4
GROUP_DIM_B = 128
D_B = N_GROUPS_B * GROUP_DIM_B
NEG_INF = -1e30

LANES = 128
HEADS_PER_SLAB = LANES // HEAD_DIM_A
N_SLABS = N_HEADS_A // HEADS_PER_SLAB

TILE = 512
Q_GROUP_CHUNKS = 4
Q_GROUP = Q_GROUP_CHUNKS * CHUNK
KEY_WINDOW = (Q_GROUP_CHUNKS + N_PREV_CHUNKS) * CHUNK
PREV = N_PREV_CHUNKS * CHUNK
N_Q_GROUPS = TILE // Q_GROUP
REL_RING = Q_GROUP + KEY_WINDOW
SLAB_Q = HEADS_PER_SLAB * Q_GROUP

OFF_Q = 0
OFF_K = OFF_Q + D_A
OFF_V = OFF_K + D_A
OFF_GA = OFF_V + D_A
OFF_UB = OFF_GA + D_A
OFF_VB = OFF_UB + D_B
OFF_GB = OFF_VB + D_B
OFF_GATE_A = OFF_GB + D_B
OFF_GATE_B = OFF_GATE_A + D_MODEL
D_IN = OFF_GATE_B + D_MODEL
HALF = D_MODEL // 2

VMEM_LIMIT_BYTES = 60 * 1024 * 1024

_NT = (((1,), (1,)), ((), ()))


def _proj(h, w_ref, off, width):
    return jnp.dot(h, w_ref[:, off:off + width], preferred_element_type=jnp.float32)


def _block_kernel(x_ref, ng_ref, win_ref, wvt_ref, bgate_ref, relb_ref, lng_ref, lnb_ref, ws_ref,
                  bs_ref, wpa_ref, wpb_ref, wout_ref, fg_ref, o_ref,
                  h_ref, q_ref, k_ref, vt_ref, ot_ref, bias_ref,
                  sc_ref, gpa_ref, u_ref, vn_ref, yb_ref, ga_ref, gpb_ref):
    t = pl.program_id(1)
    bf16 = jnp.bfloat16
    f32 = jnp.float32

    @pl.when((pl.program_id(0) == 0) & (t == 0))
    def _():
        kc = lax.broadcasted_iota(jnp.int32, (KEY_WINDOW, Q_GROUP), 0) // CHUNK
        qc = lax.broadcasted_iota(jnp.int32, (KEY_WINDOW, Q_GROUP), 1) // CHUNK
        band = (kc >= qc) & (kc <= qc + N_PREV_CHUNKS)
        for hd in range(N_HEADS_A):
            base = jnp.broadcast_to(relb_ref[hd:hd + 1, :], (KEY_WINDOW, REL_RING))
            toeplitz = pltpu.roll(base, 0, 1, stride=1, stride_axis=0)
            lo = (hd % HEADS_PER_SLAB) * Q_GROUP
            bias_ref[hd // HEADS_PER_SLAB, :, lo:lo + Q_GROUP] = jnp.where(
                band, toeplitz[:, :Q_GROUP], NEG_INF)

    @pl.when(t == 0)
    def _():
        k_ref[:PREV, :] = jnp.zeros((PREV, D_A), bf16)
        vt_ref[:, :PREV] = jnp.zeros((D_A, PREV), bf16)

    x = x_ref[0]
    inv = lax.rsqrt(jnp.mean(x * x, axis=-1, keepdims=True) + EPS)
    h_ref[...] = (x * inv * ng_ref[...]).astype(bf16)

    def proj(off, width):
        return _proj(h_ref[...], win_ref, off, width)

    q_ref[...] = (proj(OFF_Q, D_A) * (HEAD_DIM_A ** -0.5)).astype(bf16)
    k_ref[PREV:, :] = proj(OFF_K, D_A).astype(bf16)
    vt_ref[:, PREV:] = lax.dot_general(wvt_ref[...], h_ref[...], _NT, preferred_element_type=f32).astype(bf16)

    def gate_path_a():
        gpa_ref[...] = jax.nn.silu(proj(OFF_GA, D_A))

    def sgu_values():
        vb = jax.nn.gelu(proj(OFF_VB, D_B))
        mu = jnp.mean(vb, axis=-1, keepdims=True)
        var = jnp.mean(jnp.square(vb - mu), axis=-1, keepdims=True)
        vn_ref[...] = ((vb - mu) * lax.rsqrt(var + EPS) * lng_ref[...] + lnb_ref[...]).astype(bf16)

    def sgu_inputs():
        u_ref[...] = jax.nn.gelu(proj(OFF_UB, D_B))

    def sgu_mix():
        row = lax.broadcasted_iota(jnp.int32, (SGU_CHUNK, SGU_CHUNK), 0)
        col = lax.broadcasted_iota(jnp.int32, (SGU_CHUNK, SGU_CHUNK), 1)
        causal = row >= col
        n_chunks = TILE // SGU_CHUNK
        gate_path_b = jax.nn.silu(proj(OFF_GB, D_B))
        for gi in range(N_GROUPS_B):
            w_g = jnp.where(causal, ws_ref[gi], 0.0).astype(bf16)
            c0 = gi * GROUP_DIM_B
            rhs = jnp.concatenate([vn_ref[n * SGU_CHUNK:(n + 1) * SGU_CHUNK, c0:c0 + GROUP_DIM_B]
                                   for n in range(n_chunks)], axis=1)
            res = jnp.dot(w_g, rhs, preferred_element_type=f32)
            mixed = jnp.concatenate(
                [res[:, n * GROUP_DIM_B:(n + 1) * GROUP_DIM_B] + bs_ref[gi] for n in range(n_chunks)], axis=0)
            yb_ref[:, c0:c0 + GROUP_DIM_B] = (
                u_ref[:, c0:c0 + GROUP_DIM_B] * mixed * gate_path_b[:, c0:c0 + GROUP_DIM_B]).astype(bf16)

    def branch_b_proj():
        gpb_ref[...] = jnp.dot(yb_ref[...], wpb_ref[...], preferred_element_type=f32)

    def merge_gate_a(i):
        def piece():
            lo = i * HALF
            ga_ref[:, lo:lo + HALF] = jax.nn.sigmoid(
                proj(OFF_GATE_A + lo, HALF) + bgate_ref[:, lo:lo + HALF])
        return piece

    def merge_gate_b(i):
        def piece():
            lo = i * HALF
            gb = jax.nn.sigmoid(proj(OFF_GATE_B + lo, HALF) + bgate_ref[:, D_MODEL + lo:D_MODEL + lo + HALF])
            gpb_ref[:, lo:lo + HALF] = gb * gpb_ref[:, lo:lo + HALF]
        return piece

    pieces = [gate_path_a, sgu_values, sgu_inputs, sgu_mix, branch_b_proj,
              merge_gate_a(0), merge_gate_a(1), merge_gate_b(0)]
    last_piece = merge_gate_b(1)

    lane = lax.broadcasted_iota(jnp.int32, (Q_GROUP, LANES), 1)
    first_head = lane < HEAD_DIM_A
    key_row = lax.broadcasted_iota(jnp.int32, (KEY_WINDOW, LANES), 0)

    def scores(i, g, s):
        r0, c0 = g * Q_GROUP, s * LANES
        q2 = q_ref[r0:r0 + Q_GROUP, c0:c0 + LANES]
        zero = jnp.zeros_like(q2)
        qm = jnp.concatenate([jnp.where(first_head, q2, zero),
                              jnp.where(first_head, zero, q2)], axis=0)
        ks = k_ref[r0:r0 + KEY_WINDOW, c0:c0 + LANES]
        sc_ref[i % 2] = lax.dot_general(ks, qm, _NT, preferred_element_type=f32)

    def softmax_out(i, g, s):
        r0 = g * Q_GROUP
        pen = jnp.where((t == 0) & (key_row < PREV - r0), NEG_INF, 0.0).astype(f32)
        pen = jnp.concatenate([pen] * (SLAB_Q // LANES), axis=1)
        sc = sc_ref[i % 2] + bias_ref[s] + pen
        m = jnp.max(sc, axis=0, keepdims=True)
        e = jnp.exp(sc - m)
        l = jnp.sum(e, axis=0, keepdims=True)
        pt = e.astype(bf16)
        for j in range(HEADS_PER_SLAB):
            d0 = (HEADS_PER_SLAB * s + j) * HEAD_DIM_A
            q0 = j * Q_GROUP
            vt = vt_ref[d0:d0 + HEAD_DIM_A, r0:r0 + KEY_WINDOW]
            o = jnp.dot(vt, pt[:, q0:q0 + Q_GROUP], preferred_element_type=f32)
            ot_ref[d0:d0 + HEAD_DIM_A, r0:r0 + Q_GROUP] = o / l[:, q0:q0 + Q_GROUP]

    units = [(g, s) for g in range(N_Q_GROUPS) for s in range(N_SLABS)]
    assert len(pieces) == len(units)
    scores(0, *units[0])
    for i, unit in enumerate(units):
        if i + 1 < len(units):
            scores(i + 1, *units[i + 1])
        pieces[i]()
        softmax_out(i, *unit)

    k_ref[:PREV, :] = k_ref[PREV:, :]
    vt_ref[:, :PREV] = vt_ref[:, PREV:]

    last_piece()
    y_a = (ot_ref[...].T * gpa_ref[...]).astype(bf16)
    p_a = jnp.dot(y_a, wpa_ref[...], preferred_element_type=f32)
    merged = (ga_ref[...] * p_a + gpb_ref[...]).astype(bf16)
    y = x_ref[0] + jnp.dot(merged, wout_ref[...], preferred_element_type=f32)
    inv_y = lax.rsqrt(jnp.mean(y * y, axis=-1, keepdims=True) + EPS)
    o_ref[0] = y * inv_y * fg_ref[...]


def _rel_ring(rel_bias):
    heads, n_rel = rel_bias.shape
    far = lambda n: jnp.broadcast_to(rel_bias[:, -1:], (heads, n))
    near = lambda n: jnp.broadcast_to(rel_bias[:, :1], (heads, n))
    n_near = REL_RING - PREV - REL_CLIP - (Q_GROUP + 1)
    n_tail = REL_RING - (Q_GROUP + 1) - n_near - n_rel
    ring = jnp.concatenate([far(Q_GROUP + 1), near(n_near), rel_bias, far(n_tail)], axis=1)
    return ring.astype(jnp.float32)


def _const_spec(shape):
    zeros = (0,) * len(shape)
    return pl.BlockSpec(shape, lambda b, t: zeros, pipeline_mode=pl.Buffered(1))


@jax.jit
def kernel(x, norm_g, w_in, b_gate, rel_bias, sgu_ln_g, sgu_ln_b, w_s, b_s, w_pa, w_pb, w_out, final_g):
    batch, seq, d = x.shape
    assert d == D_MODEL and seq % TILE == 0 and norm_g.shape[0] == 1
    bf16 = jnp.bfloat16
    f32 = jnp.float32
    row = lambda a: a.reshape(1, -1).astype(f32)
    bs_b = jnp.broadcast_to(b_s[0][:, :, None], (N_GROUPS_B, SGU_CHUNK, GROUP_DIM_B)).astype(f32)
    w_in_bf = w_in[0].astype(bf16)
    operands = (
        x, row(norm_g[0]), w_in_bf, w_in_bf[:, OFF_V:OFF_V + D_A].T, row(b_gate[0]), _rel_ring(rel_bias[0]),
        row(sgu_ln_g[0]), row(sgu_ln_b[0]), w_s[0], bs_b,
        w_pa[0].astype(bf16), w_pb[0].astype(bf16), w_out[0].astype(bf16), row(final_g),
    )
    x_spec = pl.BlockSpec((1, TILE, D_MODEL), lambda b, t: (b, t, 0))
    in_specs = [x_spec] + [_const_spec(a.shape) for a in operands[1:]]
    return pl.pallas_call(
        _block_kernel,
        grid=(batch, seq // TILE),
        in_specs=in_specs,
        out_specs=pl.BlockSpec((1, TILE, D_MODEL), lambda b, t: (b, t, 0)),
        out_shape=jax.ShapeDtypeStruct(x.shape, x.dtype),
        scratch_shapes=[
            pltpu.VMEM((TILE, D_MODEL), bf16),
            pltpu.VMEM((TILE, D_A), bf16),
            pltpu.VMEM((PREV + TILE, D_A), bf16),
            pltpu.VMEM((D_A, PREV + TILE), bf16),
            pltpu.VMEM((D_A, TILE), f32),
            pltpu.VMEM((N_SLABS, KEY_WINDOW, SLAB_Q), f32),
            pltpu.VMEM((2, KEY_WINDOW, SLAB_Q), f32),
            pltpu.VMEM((TILE, D_A), f32),
            pltpu.VMEM((TILE, D_B), f32),
            pltpu.VMEM((TILE, D_B), bf16),
            pltpu.VMEM((TILE, D_B), bf16),
            pltpu.VMEM((TILE, D_MODEL), f32),
            pltpu.VMEM((TILE, D_MODEL), f32),
        ],
        compiler_params=pltpu.CompilerParams(
            dimension_semantics=("arbitrary", "arbitrary"),
            vmem_limit_bytes=VMEM_LIMIT_BYTES,
        ),
        name="hybrid_block",
    )(*operands)
```

```python
import jax
import jax.numpy as jnp
from jax import lax
from jax.experimental import pallas as pl
from jax.experimental.pallas import tpu as pltpu

D_MODEL = 1024
CHUNK = 64
EPS = 1e-6
N_HEADS_A = 8
HEAD_DIM_A = 64
D_A = N_HEADS_A * HEAD_DIM_A
N_PREV_CHUNKS = 8
REL_CLIP = 128
SGU_CHUNK = 128
N_GROUPS_B = 4
GROUP_DIM_B = 128
D_B = N_GROUPS_B * GROUP_DIM_B
NEG_INF = -1e30

LANES = 128
HEADS_PER_SLAB = LANES // HEAD_DIM_A
N_SLABS = N_HEADS_A // HEADS_PER_SLAB

TILE = 512
Q_GROUP_CHUNKS = 4
Q_GROUP = Q_GROUP_CHUNKS * CHUNK
KEY_WINDOW = (Q_GROUP_CHUNKS + N_PREV_CHUNKS) * CHUNK
PREV = N_PREV_CHUNKS * CHUNK
N_Q_GROUPS = TILE // Q_GROUP
REL_RING = Q_GROUP + KEY_WINDOW
SLAB_Q = HEADS_PER_SLAB * Q_GROUP

OFF_Q = 0
OFF_K = OFF_Q + D_A
OFF_V = OFF_K + D_A
OFF_GA = OFF_V + D_A
OFF_UB = OFF_GA + D_A
OFF_VB = OFF_UB + D_B
OFF_GB = OFF_VB + D_B
OFF_GATE_A = OFF_GB + D_B
OFF_GATE_B = OFF_GATE_A + D_MODEL
D_IN = OFF_GATE_B + D_MODEL
HALF = D_MODEL // 2

VMEM_LIMIT_BYTES = 60 * 1024 * 1024

_NT = (((1,), (1,)), ((), ()))


def _block_kernel(x_ref, ng_ref, win_ref, bgate_ref, relb_ref, lng_ref, lnb_ref, ws_ref,
                  bs_ref, wpa_ref, wpb_ref, wout_ref, fg_ref, o_ref,
                  h_ref, q_ref, k_ref, v_ref, ya_ref, bias_ref,
                  sc_ref, gpa_ref, u_ref, vn_ref, yb_ref, ga_ref, gpb_ref):
    t = pl.program_id(1)
    bf16 = jnp.bfloat16
    f32 = jnp.float32

    @pl.when((pl.program_id(0) == 0) & (t == 0))
    def _():
        qc = lax.broadcasted_iota(jnp.int32, (Q_GROUP, KEY_WINDOW), 0) // CHUNK
        kc = lax.broadcasted_iota(jnp.int32, (Q_GROUP, KEY_WINDOW), 1) // CHUNK
        band = (kc >= qc) & (kc <= qc + N_PREV_CHUNKS)
        for hd in range(N_HEADS_A):
            base = jnp.broadcast_to(relb_ref[hd:hd + 1, :], (Q_GROUP, REL_RING))
            toeplitz = pltpu.roll(base, 0, 1, stride=1, stride_axis=0)
            lo = (hd % HEADS_PER_SLAB) * Q_GROUP
            bias_ref[hd // HEADS_PER_SLAB, lo:lo + Q_GROUP, :] = jnp.where(
                band, toeplitz[:, :KEY_WINDOW], NEG_INF)

    @pl.when(t == 0)
    def _():
        k_ref[:PREV, :] = jnp.zeros((PREV, D_A), bf16)
        v_ref[:PREV, :] = jnp.zeros((PREV, D_A), bf16)

    x = x_ref[0]
    inv = lax.rsqrt(jnp.mean(x * x, axis=-1, keepdims=True) + EPS)
    h_ref[...] = (x * inv * ng_ref[...]).astype(bf16)

    def proj(off, width):
        return jnp.dot(h_ref[...], win_ref[:, off:off + width], preferred_element_type=f32)

    q_ref[...] = (proj(OFF_Q, D_A) * (HEAD_DIM_A ** -0.5)).astype(bf16)
    k_ref[PREV:, :] = proj(OFF_K, D_A).astype(bf16)
    v_ref[PREV:, :] = proj(OFF_V, D_A).astype(bf16)

    def gate_path_a():
        gpa_ref[...] = jax.nn.silu(proj(OFF_GA, D_A))

    def sgu_values():
        vb = jax.nn.gelu(proj(OFF_VB, D_B))
        mu = jnp.mean(vb, axis=-1, keepdims=True)
        var = jnp.mean(jnp.square(vb - mu), axis=-1, keepdims=True)
        vn_ref[...] = ((vb - mu) * lax.rsqrt(var + EPS) * lng_ref[...] + lnb_ref[...]).astype(bf16)

    def sgu_inputs():
        u_ref[...] = jax.nn.gelu(proj(OFF_UB, D_B))

    def sgu_mix():
        row = lax.broadcasted_iota(jnp.int32, (SGU_CHUNK, SGU_CHUNK), 0)
        col = lax.broadcasted_iota(jnp.int32, (SGU_CHUNK, SGU_CHUNK), 1)
        causal = row >= col
        n_chunks = TILE // SGU_CHUNK
        gate_path_b = jax.nn.silu(proj(OFF_GB, D_B))
        for gi in range(N_GROUPS_B):
            w_g = jnp.where(causal, ws_ref[gi], 0.0).astype(bf16)
            c0 = gi * GROUP_DIM_B
            rhs = jnp.concatenate([vn_ref[n * SGU_CHUNK:(n + 1) * SGU_CHUNK, c0:c0 + GROUP_DIM_B]
                                   for n in range(n_chunks)], axis=1)
            res = jnp.dot(w_g, rhs, preferred_element_type=f32)
            mixed = jnp.concatenate(
                [res[:, n * GROUP_DIM_B:(n + 1) * GROUP_DIM_B] + bs_ref[gi] for n in range(n_chunks)], axis=0)
            yb_ref[:, c0:c0 + GROUP_DIM_B] = (
                u_ref[:, c0:c0 + GROUP_DIM_B] * mixed * gate_path_b[:, c0:c0 + GROUP_DIM_B]).astype(bf16)

    def branch_b_proj():
        gpb_ref[...] = jnp.dot(yb_ref[...], wpb_ref[...], preferred_element_type=f32)

    def merge_gate_a(i):
        def piece():
            lo = i * HALF
            ga_ref[:, lo:lo + HALF] = jax.nn.sigmoid(
                proj(OFF_GATE_A + lo, HALF) + bgate_ref[:, lo:lo + HALF])
        return piece

    def merge_gate_b(i):
        def piece():
            lo = i * HALF
            gb = jax.nn.sigmoid(proj(OFF_GATE_B + lo, HALF) + bgate_ref[:, D_MODEL + lo:D_MODEL + lo + HALF])
            gpb_ref[:, lo:lo + HALF] = gb * gpb_ref[:, lo:lo + HALF]
        return piece

    pieces = [gate_path_a, sgu_values, sgu_inputs, sgu_mix, branch_b_proj,
              merge_gate_a(0), merge_gate_a(1), merge_gate_b(0)]
    last_piece = merge_gate_b(1)

    lane = lax.broadcasted_iota(jnp.int32, (Q_GROUP, LANES), 1)
    first_head = lane < HEAD_DIM_A
    key_col = lax.broadcasted_iota(jnp.int32, (1, KEY_WINDOW), 1)

    def scores(i, g, s):
        r0, c0 = g * Q_GROUP, s * LANES
        q2 = q_ref[r0:r0 + Q_GROUP, c0:c0 + LANES]
        zero = jnp.zeros_like(q2)
        qm = jnp.concatenate([jnp.where(first_head, q2, zero),
                              jnp.where(first_head, zero, q2)], axis=0)
        ks = k_ref[r0:r0 + KEY_WINDOW, c0:c0 + LANES]
        sc_ref[i % 2] = lax.dot_general(qm, ks, _NT, preferred_element_type=f32)

    def softmax_out(i, g, s):
        r0, c0 = g * Q_GROUP, s * LANES
        pen = jnp.where((t == 0) & (key_col < PREV - r0), NEG_INF, 0.0).astype(f32)
        sc = sc_ref[i % 2] + bias_ref[s] + pen
        m = jnp.max(sc, axis=-1, keepdims=True)
        e = jnp.exp(sc - m)
        l = jnp.sum(e, axis=-1, keepdims=True)
        vs = v_ref[r0:r0 + KEY_WINDOW, c0:c0 + LANES]
        o = jnp.dot(e.astype(bf16), vs, preferred_element_type=f32) / l
        o2 = jnp.where(first_head, o[:Q_GROUP], o[Q_GROUP:])
        ya_ref[r0:r0 + Q_GROUP, c0:c0 + LANES] = (
            o2 * gpa_ref[r0:r0 + Q_GROUP, c0:c0 + LANES]).astype(bf16)

    units = [(g, s) for g in range(N_Q_GROUPS) for s in range(N_SLABS)]
    assert len(pieces) == len(units)
    scores(0, *units[0])
    for i, unit in enumerate(units):
        if i + 1 < len(units):
            scores(i + 1, *units[i + 1])
        pieces[i]()
        softmax_out(i, *unit)

    k_ref[:PREV, :] = k_ref[PREV:, :]
    v_ref[:PREV, :] = v_ref[PREV:, :]

    last_piece()
    p_a = jnp.dot(ya_ref[...], wpa_ref[...], preferred_element_type=f32)
    merged = (ga_ref[...] * p_a + gpb_ref[...]).astype(bf16)
    y = x_ref[0] + jnp.dot(merged, wout_ref[...], preferred_element_type=f32)
    inv_y = lax.rsqrt(jnp.mean(y * y, axis=-1, keepdims=True) + EPS)
    o_ref[0] = y * inv_y * fg_ref[...]


def _rel_ring(rel_bias):
    heads = rel_bias.shape[0]
    far = lambda n: jnp.broadcast_to(rel_bias[:, -1:], (heads, n))
    near = lambda n: jnp.broadcast_to(rel_bias[:, :1], (heads, n))
    n_far = PREV - REL_CLIP
    n_near = KEY_WINDOW - n_far - rel_bias.shape[1]
    ring = jnp.concatenate([far(n_far), rel_bias[:, ::-1], near(n_near), far(REL_RING - KEY_WINDOW)], axis=1)
    return ring.astype(jnp.float32)


def _const_spec(shape):
    zeros = (0,) * len(shape)
    return pl.BlockSpec(shape, lambda b, t: zeros, pipeline_mode=pl.Buffered(1))


@jax.jit
def kernel(x, norm_g, w_in, b_gate, rel_bias, sgu_ln_g, sgu_ln_b, w_s, b_s, w_pa, w_pb, w_out, final_g):
    batch, seq, d = x.shape
    assert d == D_MODEL and seq % TILE == 0 and norm_g.shape[0] == 1
    bf16 = jnp.bfloat16
    f32 = jnp.float32
    row = lambda a: a.reshape(1, -1).astype(f32)
    bs_b = jnp.broadcast_to(b_s[0][:, :, None], (N_GROUPS_B, SGU_CHUNK, GROUP_DIM_B)).astype(f32)
    operands = (
        x, row(norm_g[0]), w_in[0].astype(bf16), row(b_gate[0]), _rel_ring(rel_bias[0]),
        row(sgu_ln_g[0]), row(sgu_ln_b[0]), w_s[0], bs_b,
        w_pa[0].astype(bf16), w_pb[0].astype(bf16), w_out[0].astype(bf16), row(final_g),
    )
    x_spec = pl.BlockSpec((1, TILE, D_MODEL), lambda b, t: (b, t, 0))
    in_specs = [x_spec] + [_const_spec(a.shape) for a in operands[1:]]
    return pl.pallas_call(
        _block_kernel,
        grid=(batch, seq // TILE),
        in_specs=in_specs,
        out_specs=pl.BlockSpec((1, TILE, D_MODEL), lambda b, t: (b, t, 0)),
        out_shape=jax.ShapeDtypeStruct(x.shape, x.dtype),
        scratch_shapes=[
            pltpu.VMEM((TILE, D_MODEL), bf16),
            pltpu.VMEM((TILE, D_A), bf16),
            pltpu.VMEM((PREV + TILE, D_A), bf16),
            pltpu.VMEM((PREV + TILE, D_A), bf16),
            pltpu.VMEM((TILE, D_A), bf16),
            pltpu.VMEM((N_SLABS, SLAB_Q, KEY_WINDOW), f32),
            pltpu.VMEM((2, SLAB_Q, KEY_WINDOW), f32),
            pltpu.VMEM((TILE, D_A), f32),
            pltpu.VMEM((TILE, D_B), f32),
            pltpu.VMEM((TILE, D_B), bf16),
            pltpu.VMEM((TILE, D_B), bf16),
            pltpu.VMEM((TILE, D_MODEL), f32),
            pltpu.VMEM((TILE, D_MODEL), f32),
        ],
        compiler_params=pltpu.CompilerParams(
            dimension_semantics=("arbitrary", "arbitrary"),
            vmem_limit_bytes=VMEM_LIMIT_BYTES,
        ),
        name="hybrid_block",
    )(*operands)
```

```python
import jax
import jax.numpy as jnp
from jax import lax
from jax.experimental import pallas as pl
from jax.experimental.pallas import tpu as pltpu

D_MODEL = 1024
CHUNK = 64
EPS = 1e-6
N_HEADS_A = 8
HEAD_DIM_A = 64
D_A = N_HEADS_A * HEAD_DIM_A
N_PREV_CHUNKS = 8
REL_CLIP = 128
SGU_CHUNK = 128
N_GROUPS_B = 4
GROUP_DIM_B = 128
D_B = N_GROUPS_B * GROUP_DIM_B
NEG_INF = -1e30

LANES = 128
HEADS_PER_SLAB = LANES // HEAD_DIM_A
N_SLABS = N_HEADS_A // HEADS_PER_SLAB

TILE = 512
ROW_SPLIT = 2
Q_GROUP_CHUNKS = 4
Q_GROUP = Q_GROUP_CHUNKS * CHUNK
KEY_WINDOW = (Q_GROUP_CHUNKS + N_PREV_CHUNKS) * CHUNK
PREV = N_PREV_CHUNKS * CHUNK
N_Q_GROUPS = TILE // Q_GROUP
REL_RING = Q_GROUP + KEY_WINDOW
SLAB_Q = HEADS_PER_SLAB * Q_GROUP
SCORE_SLOTS = 3
PROB_SLOTS = 2
KEY_ROWS = KEY_WINDOW // ROW_SPLIT

OFF_Q = 0
OFF_K = OFF_Q + D_A
OFF_V = OFF_K + D_A
OFF_GA = OFF_V + D_A
OFF_UB = OFF_GA + D_A
OFF_VB = OFF_UB + D_B
OFF_GB = OFF_VB + D_B
OFF_GATE_A = OFF_GB + D_B
OFF_GATE_B = OFF_GATE_A + D_MODEL
D_IN = OFF_GATE_B + D_MODEL
HALF = D_MODEL // 2

VMEM_LIMIT_BYTES = 60 * 1024 * 1024

_NT = (((1,), (1,)), ((), ()))


def _block_kernel(x_ref, ng_ref, win_ref, bgate_ref, relb_ref, lng_ref, lnb_ref, ws_ref,
                  bs_ref, wpa_ref, wpb_ref, wout_ref, fg_ref, o_ref,
                  h_ref, q_ref, k_ref, vt_ref, ot_ref, bias_ref, wvt_ref,
                  sc_ref, p_ref, gpa_ref, u_ref, vn_ref, yb_ref, ga_ref, gpb_ref):
    t = pl.program_id(1)
    bf16 = jnp.bfloat16
    f32 = jnp.float32

    @pl.when((pl.program_id(0) == 0) & (t == 0))
    def _():
        kc = lax.broadcasted_iota(jnp.int32, (KEY_WINDOW, Q_GROUP), 0) // CHUNK
        qc = lax.broadcasted_iota(jnp.int32, (KEY_WINDOW, Q_GROUP), 1) // CHUNK
        band = (kc >= qc) & (kc <= qc + N_PREV_CHUNKS)
        for hd in range(N_HEADS_A):
            base = jnp.broadcast_to(relb_ref[hd:hd + 1, :], (KEY_WINDOW, REL_RING))
            toeplitz = pltpu.roll(base, 0, 1, stride=1, stride_axis=0)
            lo = (hd % HEADS_PER_SLAB) * Q_GROUP
            bias_ref[hd // HEADS_PER_SLAB, :, lo:lo + Q_GROUP] = jnp.where(
                band, toeplitz[:, :Q_GROUP], NEG_INF)
        wvt_ref[...] = win_ref[:, OFF_V:OFF_V + D_A].astype(f32).T.astype(bf16)

    @pl.when(t == 0)
    def _():
        k_ref[:PREV, :] = jnp.zeros((PREV, D_A), bf16)
        vt_ref[:, :PREV] = jnp.zeros((D_A, PREV), bf16)

    rows = TILE // ROW_SPLIT
    for r0 in range(0, TILE, rows):
        x = x_ref[0, r0:r0 + rows, :]
        inv = lax.rsqrt(jnp.mean(x * x, axis=-1, keepdims=True) + EPS)
        h_ref[r0:r0 + rows, :] = (x * inv * ng_ref[...]).astype(bf16)
        q = jnp.dot(h_ref[r0:r0 + rows, :], win_ref[:, OFF_Q:OFF_Q + D_A], preferred_element_type=f32)
        q_ref[r0:r0 + rows, :] = (q * (HEAD_DIM_A ** -0.5)).astype(bf16)

    row_starts = range(0, TILE, rows)

    def proj(r0, off, width):
        return jnp.dot(h_ref[r0:r0 + rows, :], win_ref[:, off:off + width], preferred_element_type=f32)

    for r0 in row_starts:
        k_ref[PREV + r0:PREV + r0 + rows, :] = proj(r0, OFF_K, D_A).astype(bf16)
    for d0 in range(0, D_A, rows):
        vt_ref[d0:d0 + rows, PREV:] = lax.dot_general(
            wvt_ref[d0:d0 + rows, :], h_ref[...], _NT, preferred_element_type=f32).astype(bf16)

    def gate_path_a(r0):
        gpa_ref[r0:r0 + rows, :] = jax.nn.silu(proj(r0, OFF_GA, D_A))

    def sgu_values(r0):
        vb = jax.nn.gelu(proj(r0, OFF_VB, D_B))
        mu = jnp.mean(vb, axis=-1, keepdims=True)
        var = jnp.mean(jnp.square(vb - mu), axis=-1, keepdims=True)
        vn_ref[r0:r0 + rows, :] = (
            (vb - mu) * lax.rsqrt(var + EPS) * lng_ref[...] + lnb_ref[...]).astype(bf16)

    def sgu_inputs(r0):
        u_ref[r0:r0 + rows, :] = jax.nn.gelu(proj(r0, OFF_UB, D_B))

    def sgu_gate(r0):
        gpb_ref[r0:r0 + rows, :D_B] = jax.nn.silu(proj(r0, OFF_GB, D_B))

    def sgu_mix():
        row = lax.broadcasted_iota(jnp.int32, (SGU_CHUNK, SGU_CHUNK), 0)
        col = lax.broadcasted_iota(jnp.int32, (SGU_CHUNK, SGU_CHUNK), 1)
        causal = row >= col
        n_chunks = TILE // SGU_CHUNK
        gate_path_b = gpb_ref[:, :D_B]
        for gi in range(N_GROUPS_B):
            w_g = jnp.where(causal, ws_ref[gi], 0.0).astype(bf16)
            c0 = gi * GROUP_DIM_B
            rhs = jnp.concatenate([vn_ref[n * SGU_CHUNK:(n + 1) * SGU_CHUNK, c0:c0 + GROUP_DIM_B]
                                   for n in range(n_chunks)], axis=1)
            res = jnp.dot(w_g, rhs, preferred_element_type=f32)
            mixed = jnp.concatenate(
                [res[:, n * GROUP_DIM_B:(n + 1) * GROUP_DIM_B] + bs_ref[gi] for n in range(n_chunks)], axis=0)
            yb_ref[:, c0:c0 + GROUP_DIM_B] = (
                u_ref[:, c0:c0 + GROUP_DIM_B] * mixed * gate_path_b[:, c0:c0 + GROUP_DIM_B]).astype(bf16)

    def branch_b_proj(r0):
        gpb_ref[r0:r0 + rows, :] = jnp.dot(yb_ref[r0:r0 + rows, :], wpb_ref[...], preferred_element_type=f32)

    def merge_gate_a(r0, lo):
        ga_ref[r0:r0 + rows, lo:lo + HALF] = jax.nn.sigmoid(
            proj(r0, OFF_GATE_A + lo, HALF) + bgate_ref[:, lo:lo + HALF])

    def merge_gate_b(r0, lo):
        gb = jax.nn.sigmoid(
            proj(r0, OFF_GATE_B + lo, HALF) + bgate_ref[:, D_MODEL + lo:D_MODEL + lo + HALF])
        gpb_ref[r0:r0 + rows, lo:lo + HALF] = gb * gpb_ref[r0:r0 + rows, lo:lo + HALF]

    def both_rows(fn, *args):
        return [lambda r0=r0: fn(r0, *args) for r0 in row_starts]

    fillers = (both_rows(gate_path_a) + both_rows(sgu_values) + both_rows(sgu_inputs) + both_rows(sgu_gate)
               + [sgu_mix] + both_rows(merge_gate_a, 0) + both_rows(branch_b_proj)
               + both_rows(merge_gate_a, HALF) + both_rows(merge_gate_b, 0) + both_rows(merge_gate_b, HALF))

    lane = lax.broadcasted_iota(jnp.int32, (Q_GROUP, LANES), 1)
    first_head = lane < HEAD_DIM_A
    key_row = lax.broadcasted_iota(jnp.int32, (KEY_WINDOW, LANES), 0)
    units = [(g, s) for g in range(N_Q_GROUPS) for s in range(N_SLABS)]
    inv_sums = {}

    def scores(i, k0):
        g, s = units[i]
        r0, c0 = g * Q_GROUP, s * LANES
        q2 = q_ref[r0:r0 + Q_GROUP, c0:c0 + LANES]
        zero = jnp.zeros_like(q2)
        qm = jnp.concatenate([jnp.where(first_head, q2, zero),
                              jnp.where(first_head, zero, q2)], axis=0)
        ks = k_ref[r0 + k0:r0 + k0 + KEY_ROWS, c0:c0 + LANES]
        sc_ref[i % SCORE_SLOTS, k0:k0 + KEY_ROWS, :] = lax.dot_general(ks, qm, _NT, preferred_element_type=f32)

    def softmax(i):
        g, s = units[i]
        pen = jnp.where((t == 0) & (key_row < PREV - g * Q_GROUP), NEG_INF, 0.0).astype(f32)
        pen = jnp.concatenate([pen] * (SLAB_Q // LANES), axis=1)
        sc = sc_ref[i % SCORE_SLOTS] + bias_ref[s] + pen
        m = jnp.max(sc, axis=0, keepdims=True)
        e = jnp.exp(sc - m)
        inv_sums[i] = 1.0 / jnp.sum(e, axis=0, keepdims=True)
        p_ref[i % PROB_SLOTS] = e.astype(bf16)

    def attend(i):
        g, s = units[i]
        r0 = g * Q_GROUP
        for j in range(HEADS_PER_SLAB):
            d0 = (HEADS_PER_SLAB * s + j) * HEAD_DIM_A
            q0 = j * Q_GROUP
            vt = vt_ref[d0:d0 + HEAD_DIM_A, r0:r0 + KEY_WINDOW]
            o = jnp.dot(vt, p_ref[i % PROB_SLOTS, :, q0:q0 + Q_GROUP], preferred_element_type=f32)
            ot_ref[d0:d0 + HEAD_DIM_A, r0:r0 + Q_GROUP] = o * inv_sums[i][:, q0:q0 + Q_GROUP]

    pending = iter(fillers)

    def issue(n_scores_unit):
        for k0 in range(0, KEY_WINDOW, KEY_ROWS):
            if n_scores_unit < len(units):
                scores(n_scores_unit, k0)
            filler = next(pending, None)
            if filler is not None:
                filler()

    issue(0)
    issue(1)
    for i in range(len(units)):
        issue(i + 2)
        if i >= 1:
            attend(i - 1)
        softmax(i)
    attend(len(units) - 1)
    for filler in pending:
        filler()

    k_ref[:PREV, :] = k_ref[PREV:, :]
    vt_ref[:, :PREV] = vt_ref[:, PREV:]

    p_a = []
    for r0 in range(0, TILE, rows):
        y_a = (ot_ref[:, r0:r0 + rows].T * gpa_ref[r0:r0 + rows, :]).astype(bf16)
        p_a.append(jnp.dot(y_a, wpa_ref[...], preferred_element_type=f32))
    for n, r0 in enumerate(range(0, TILE, rows)):
        merged = (ga_ref[r0:r0 + rows, :] * p_a[n] + gpb_ref[r0:r0 + rows, :]).astype(bf16)
        y = x_ref[0, r0:r0 + rows, :] + jnp.dot(merged, wout_ref[...], preferred_element_type=f32)
        inv_y = lax.rsqrt(jnp.mean(y * y, axis=-1, keepdims=True) + EPS)
        o_ref[0, r0:r0 + rows, :] = y * inv_y * fg_ref[...]


def _rel_ring(rel_bias):
    heads, n_rel = rel_bias.shape
    far = lambda n: jnp.broadcast_to(rel_bias[:, -1:], (heads, n))
    near = lambda n: jnp.broadcast_to(rel_bias[:, :1], (heads, n))
    n_near = REL_RING - PREV - REL_CLIP - (Q_GROUP + 1)
    n_tail = REL_RING - (Q_GROUP + 1) - n_near - n_rel
    ring = jnp.concatenate([far(Q_GROUP + 1), near(n_near), rel_bias, far(n_tail)], axis=1)
    return ring.astype(jnp.float32)


def _const_spec(shape):
    zeros = (0,) * len(shape)
    return pl.BlockSpec(shape, lambda b, t: zeros, pipeline_mode=pl.Buffered(1))


@jax.jit
def kernel(x, norm_g, w_in, b_gate, rel_bias, sgu_ln_g, sgu_ln_b, w_s, b_s, w_pa, w_pb, w_out, final_g):
    batch, seq, d = x.shape
    assert d == D_MODEL and seq % TILE == 0 and norm_g.shape[0] == 1
    bf16 = jnp.bfloat16
    f32 = jnp.float32
    row = lambda a: a.reshape(1, -1).astype(f32)
    bs_b = jnp.broadcast_to(b_s[0][:, :, None], (N_GROUPS_B, SGU_CHUNK, GROUP_DIM_B)).astype(f32)
    operands = (
        x, row(norm_g[0]), w_in[0].astype(bf16), row(b_gate[0]), _rel_ring(rel_bias[0]),
        row(sgu_ln_g[0]), row(sgu_ln_b[0]), w_s[0], bs_b,
        w_pa[0].astype(bf16), w_pb[0].astype(bf16), w_out[0].astype(bf16), row(final_g),
    )
    x_spec = pl.BlockSpec((1, TILE, D_MODEL), lambda b, t: (b, t, 0))
    in_specs = [x_spec] + [_const_spec(a.shape) for a in operands[1:]]
    return pl.pallas_call(
        _block_kernel,
        grid=(batch, seq // TILE),
        in_specs=in_specs,
        out_specs=pl.BlockSpec((1, TILE, D_MODEL), lambda b, t: (b, t, 0)),
        out_shape=jax.ShapeDtypeStruct(x.shape, x.dtype),
        scratch_shapes=[
            pltpu.VMEM((TILE, D_MODEL), bf16),
            pltpu.VMEM((TILE, D_A), bf16),
            pltpu.VMEM((PREV + TILE, D_A), bf16),
            pltpu.VMEM((D_A, PREV + TILE), bf16),
            pltpu.VMEM((D_A, TILE), f32),
            pltpu.VMEM((N_SLABS, KEY_WINDOW, SLAB_Q), f32),
            pltpu.VMEM((D_A, D_MODEL), bf16),
            pltpu.VMEM((SCORE_SLOTS, KEY_WINDOW, SLAB_Q), f32),
            pltpu.VMEM((PROB_SLOTS, KEY_WINDOW, SLAB_Q), bf16),
            pltpu.VMEM((TILE, D_A), f32),
            pltpu.VMEM((TILE, D_B), f32),
            pltpu.VMEM((TILE, D_B), bf16),
            pltpu.VMEM((TILE, D_B), bf16),
            pltpu.VMEM((TILE, D_MODEL), f32),
            pltpu.VMEM((TILE, D_MODEL), f32),
        ],
        compiler_params=pltpu.CompilerParams(
            dimension_semantics=("arbitrary", "arbitrary"),
            vmem_limit_bytes=VMEM_LIMIT_BYTES,
        ),
        name="hybrid_block",
    )(*operands)
```

```python
import jax
import jax.numpy as jnp
from jax import lax
from jax.experimental import pallas as pl
from jax.experimental.pallas import tpu as pltpu

D_MODEL = 1024
CHUNK = 64
EPS = 1e-6
N_HEADS_A = 8
HEAD_DIM_A = 64
D_A = N_HEADS_A * HEAD_DIM_A
N_PREV_CHUNKS = 8
REL_CLIP = 128
SGU_CHUNK = 128
N_GROUPS_B = 4
GROUP_DIM_B = 128
D_B = N_GROUPS_B * GROUP_DIM_B
NEG_INF = -1e30

LANES = 128
HEADS_PER_SLAB = LANES // HEAD_DIM_A
N_SLABS = N_HEADS_A // HEADS_PER_SLAB

TILE = 512
ROW_SPLIT = 2
ROWS = TILE // ROW_SPLIT
Q_GROUP_CHUNKS = 4
Q_GROUP = Q_GROUP_CHUNKS * CHUNK
KEY_WINDOW = (Q_GROUP_CHUNKS + N_PREV_CHUNKS) * CHUNK
PREV = N_PREV_CHUNKS * CHUNK
N_Q_GROUPS = TILE // Q_GROUP
REL_RING = Q_GROUP + KEY_WINDOW
SLAB_Q = HEADS_PER_SLAB * Q_GROUP
SCORE_SLOTS = 3
PROB_SLOTS = 2

OFF_Q = 0
OFF_K = OFF_Q + D_A
OFF_V = OFF_K + D_A
OFF_GA = OFF_V + D_A
OFF_UB = OFF_GA + D_A
OFF_VB = OFF_UB + D_B
OFF_GB = OFF_VB + D_B
OFF_GATE_A = OFF_GB + D_B
OFF_GATE_B = OFF_GATE_A + D_MODEL
D_IN = OFF_GATE_B + D_MODEL
HALF = D_MODEL // 2

VMEM_LIMIT_BYTES = 60 * 1024 * 1024

_NT = (((1,), (1,)), ((), ()))


def _block_kernel(x_ref, ng_ref, win_ref, bgate_ref, relb_ref, lng_ref, lnb_ref, ws_ref,
                  bs_ref, wpa_ref, wpb_ref, wout_ref, fg_ref, o_ref,
                  h_ref, q_ref, k_ref, v_ref, ya_ref, bias_ref,
                  sc_ref, p_ref, gpa_ref, u_ref, vn_ref, yb_ref, ga_ref, gpb_ref):
    t = pl.program_id(1)
    bf16 = jnp.bfloat16
    f32 = jnp.float32

    @pl.when((pl.program_id(0) == 0) & (t == 0))
    def _():
        qc = lax.broadcasted_iota(jnp.int32, (Q_GROUP, KEY_WINDOW), 0) // CHUNK
        kc = lax.broadcasted_iota(jnp.int32, (Q_GROUP, KEY_WINDOW), 1) // CHUNK
        band = (kc >= qc) & (kc <= qc + N_PREV_CHUNKS)
        for hd in range(N_HEADS_A):
            base = jnp.broadcast_to(relb_ref[hd:hd + 1, :], (Q_GROUP, REL_RING))
            toeplitz = pltpu.roll(base, 0, 1, stride=1, stride_axis=0)
            lo = (hd % HEADS_PER_SLAB) * Q_GROUP
            bias_ref[hd // HEADS_PER_SLAB, lo:lo + Q_GROUP, :] = jnp.where(
                band, toeplitz[:, :KEY_WINDOW], NEG_INF)

    @pl.when(t == 0)
    def _():
        k_ref[:PREV, :] = jnp.zeros((PREV, D_A), bf16)
        v_ref[:PREV, :] = jnp.zeros((PREV, D_A), bf16)

    row_starts = range(0, TILE, ROWS)

    def proj(r0, off, width):
        return jnp.dot(h_ref[r0:r0 + ROWS, :], win_ref[:, off:off + width], preferred_element_type=f32)

    for r0 in row_starts:
        x = x_ref[0, r0:r0 + ROWS, :]
        inv = lax.rsqrt(jnp.mean(x * x, axis=-1, keepdims=True) + EPS)
        h_ref[r0:r0 + ROWS, :] = (x * inv * ng_ref[...]).astype(bf16)
        q_ref[r0:r0 + ROWS, :] = (proj(r0, OFF_Q, D_A) * (HEAD_DIM_A ** -0.5)).astype(bf16)
    for r0 in row_starts:
        k_ref[PREV + r0:PREV + r0 + ROWS, :] = proj(r0, OFF_K, D_A).astype(bf16)
    for r0 in row_starts:
        v_ref[PREV + r0:PREV + r0 + ROWS, :] = proj(r0, OFF_V, D_A).astype(bf16)

    def gate_path_a(r0):
        gpa_ref[r0:r0 + ROWS, :] = jax.nn.silu(proj(r0, OFF_GA, D_A))

    def sgu_values(r0):
        vb = jax.nn.gelu(proj(r0, OFF_VB, D_B))
        mu = jnp.mean(vb, axis=-1, keepdims=True)
        var = jnp.mean(jnp.square(vb - mu), axis=-1, keepdims=True)
        vn_ref[r0:r0 + ROWS, :] = (
            (vb - mu) * lax.rsqrt(var + EPS) * lng_ref[...] + lnb_ref[...]).astype(bf16)

    def sgu_inputs(r0):
        u_ref[r0:r0 + ROWS, :] = jax.nn.gelu(proj(r0, OFF_UB, D_B))

    def sgu_gate(r0):
        gpb_ref[r0:r0 + ROWS, :D_B] = jax.nn.silu(proj(r0, OFF_GB, D_B))

    def sgu_mix():
        row = lax.broadcasted_iota(jnp.int32, (SGU_CHUNK, SGU_CHUNK), 0)
        col = lax.broadcasted_iota(jnp.int32, (SGU_CHUNK, SGU_CHUNK), 1)
        causal = row >= col
        n_chunks = TILE // SGU_CHUNK
        gate_path_b = gpb_ref[:, :D_B]
        for gi in range(N_GROUPS_B):
            w_g = jnp.where(causal, ws_ref[gi], 0.0).astype(bf16)
            c0 = gi * GROUP_DIM_B
            rhs = jnp.concatenate([vn_ref[n * SGU_CHUNK:(n + 1) * SGU_CHUNK, c0:c0 + GROUP_DIM_B]
                                   for n in range(n_chunks)], axis=1)
            res = jnp.dot(w_g, rhs, preferred_element_type=f32)
            mixed = jnp.concatenate(
                [res[:, n * GROUP_DIM_B:(n + 1) * GROUP_DIM_B] + bs_ref[gi] for n in range(n_chunks)], axis=0)
            yb_ref[:, c0:c0 + GROUP_DIM_B] = (
                u_ref[:, c0:c0 + GROUP_DIM_B] * mixed * gate_path_b[:, c0:c0 + GROUP_DIM_B]).astype(bf16)

    def branch_b_proj(r0):
        gpb_ref[r0:r0 + ROWS, :] = jnp.dot(yb_ref[r0:r0 + ROWS, :], wpb_ref[...], preferred_element_type=f32)

    def merge_gate_a(r0, lo):
        ga_ref[r0:r0 + ROWS, lo:lo + HALF] = jax.nn.sigmoid(
            proj(r0, OFF_GATE_A + lo, HALF) + bgate_ref[:, lo:lo + HALF])

    def merge_gate_b(r0, lo):
        gb = jax.nn.sigmoid(
            proj(r0, OFF_GATE_B + lo, HALF) + bgate_ref[:, D_MODEL + lo:D_MODEL + lo + HALF])
        gpb_ref[r0:r0 + ROWS, lo:lo + HALF] = gb * gpb_ref[r0:r0 + ROWS, lo:lo + HALF]

    def both_rows(fn, *args):
        return [lambda r0=r0: fn(r0, *args) for r0 in row_starts]

    fillers = (both_rows(gate_path_a) + both_rows(sgu_values) + both_rows(sgu_inputs) + both_rows(sgu_gate)
               + [sgu_mix] + both_rows(merge_gate_a, 0) + both_rows(branch_b_proj)
               + both_rows(merge_gate_a, HALF) + both_rows(merge_gate_b, 0) + both_rows(merge_gate_b, HALF))

    lane = lax.broadcasted_iota(jnp.int32, (Q_GROUP, LANES), 1)
    first_head = lane < HEAD_DIM_A
    key_col = lax.broadcasted_iota(jnp.int32, (1, KEY_WINDOW), 1)
    units = [(g, s) for g in range(N_Q_GROUPS) for s in range(N_SLABS)]
    inv_sums = {}

    def scores(i, j):
        g, s = units[i]
        r0, c0 = g * Q_GROUP, s * LANES
        q2 = q_ref[r0:r0 + Q_GROUP, c0:c0 + LANES]
        qm = jnp.where(first_head == (j == 0), q2, jnp.zeros_like(q2))
        ks = k_ref[r0:r0 + KEY_WINDOW, c0:c0 + LANES]
        sc_ref[i % SCORE_SLOTS, j * Q_GROUP:(j + 1) * Q_GROUP, :] = lax.dot_general(
            qm, ks, _NT, preferred_element_type=f32)

    def softmax(i):
        g, s = units[i]
        pen = jnp.where((t == 0) & (key_col < PREV - g * Q_GROUP), NEG_INF, 0.0).astype(f32)
        sc = sc_ref[i % SCORE_SLOTS] + bias_ref[s] + pen
        m = jnp.max(sc, axis=-1, keepdims=True)
        e = jnp.exp(sc - m)
        inv_sums[i] = 1.0 / jnp.sum(e, axis=-1, keepdims=True)
        p_ref[i % PROB_SLOTS] = e.astype(bf16)

    def attend(i):
        g, s = units[i]
        r0, c0 = g * Q_GROUP, s * LANES
        vs = v_ref[r0:r0 + KEY_WINDOW, c0:c0 + LANES]
        o = jnp.dot(p_ref[i % PROB_SLOTS], vs, preferred_element_type=f32) * inv_sums[i]
        o2 = jnp.where(first_head, o[:Q_GROUP], o[Q_GROUP:])
        ya_ref[r0:r0 + Q_GROUP, c0:c0 + LANES] = (
            o2 * gpa_ref[r0:r0 + Q_GROUP, c0:c0 + LANES]).astype(bf16)

    pending = iter(fillers)

    def issue(n_scores_unit):
        for j in range(HEADS_PER_SLAB):
            if n_scores_unit < len(units):
                scores(n_scores_unit, j)
            filler = next(pending, None)
            if filler is not None:
                filler()

    issue(0)
    issue(1)
    for i in range(len(units)):
        issue(i + 2)
        if i >= 1:
            attend(i - 1)
        softmax(i)
    attend(len(units) - 1)
    for filler in pending:
        filler()

    k_ref[:PREV, :] = k_ref[PREV:, :]
    v_ref[:PREV, :] = v_ref[PREV:, :]

    p_a = [jnp.dot(ya_ref[r0:r0 + ROWS, :], wpa_ref[...], preferred_element_type=f32) for r0 in row_starts]
    for n, r0 in enumerate(row_starts):
        merged = (ga_ref[r0:r0 + ROWS, :] * p_a[n] + gpb_ref[r0:r0 + ROWS, :]).astype(bf16)
        y = x_ref[0, r0:r0 + ROWS, :] + jnp.dot(merged, wout_ref[...], preferred_element_type=f32)
        inv_y = lax.rsqrt(jnp.mean(y * y, axis=-1, keepdims=True) + EPS)
        o_ref[0, r0:r0 + ROWS, :] = y * inv_y * fg_ref[...]


def _rel_ring(rel_bias):
    heads = rel_bias.shape[0]
    far = lambda n: jnp.broadcast_to(rel_bias[:, -1:], (heads, n))
    near = lambda n: jnp.broadcast_to(rel_bias[:, :1], (heads, n))
    n_far = PREV - REL_CLIP
    n_near = KEY_WINDOW - n_far - rel_bias.shape[1]
    ring = jnp.concatenate([far(n_far), rel_bias[:, ::-1], near(n_near), far(REL_RING - KEY_WINDOW)], axis=1)
    return ring.astype(jnp.float32)


def _const_spec(shape):
    zeros = (0,) * len(shape)
    return pl.BlockSpec(shape, lambda b, t: zeros, pipeline_mode=pl.Buffered(1))


@jax.jit
def kernel(x, norm_g, w_in, b_gate, rel_bias, sgu_ln_g, sgu_ln_b, w_s, b_s, w_pa, w_pb, w_out, final_g):
    batch, seq, d = x.shape
    assert d == D_MODEL and seq % TILE == 0 and norm_g.shape[0] == 1
    bf16 = jnp.bfloat16
    f32 = jnp.float32
    row = lambda a: a.reshape(1, -1).astype(f32)
    bs_b = jnp.broadcast_to(b_s[0][:, :, None], (N_GROUPS_B, SGU_CHUNK, GROUP_DIM_B)).astype(f32)
    operands = (
        x, row(norm_g[0]), w_in[0].astype(bf16), row(b_gate[0]), _rel_ring(rel_bias[0]),
        row(sgu_ln_g[0]), row(sgu_ln_b[0]), w_s[0], bs_b,
        w_pa[0].astype(bf16), w_pb[0].astype(bf16), w_out[0].astype(bf16), row(final_g),
    )
    x_spec = pl.BlockSpec((1, TILE, D_MODEL), lambda b, t: (b, t, 0))
    in_specs = [x_spec] + [_const_spec(a.shape) for a in operands[1:]]
    return pl.pallas_call(
        _block_kernel,
        grid=(batch, seq // TILE),
        in_specs=in_specs,
        out_specs=pl.BlockSpec((1, TILE, D_MODEL), lambda b, t: (b, t, 0)),
        out_shape=jax.ShapeDtypeStruct(x.shape, x.dtype),
        scratch_shapes=[
            pltpu.VMEM((TILE, D_MODEL), bf16),
            pltpu.VMEM((TILE, D_A), bf16),
            pltpu.VMEM((PREV + TILE, D_A), bf16),
            pltpu.VMEM((PREV + TILE, D_A), bf16),
            pltpu.VMEM((TILE, D_A), bf16),
            pltpu.VMEM((N_SLABS, SLAB_Q, KEY_WINDOW), f32),
            pltpu.VMEM((SCORE_SLOTS, SLAB_Q, KEY_WINDOW), f32),
            pltpu.VMEM((PROB_SLOTS, SLAB_Q, KEY_WINDOW), bf16),
            pltpu.VMEM((TILE, D_A), f32),
            pltpu.VMEM((TILE, D_B), f32),
            pltpu.VMEM((TILE, D_B), bf16),
            pltpu.VMEM((TILE, D_B), bf16),
            pltpu.VMEM((TILE, D_MODEL), f32),
            pltpu.VMEM((TILE, D_MODEL), f32),
        ],
        compiler_params=pltpu.CompilerParams(
            dimension_semantics=("arbitrary", "arbitrary"),
            vmem_limit_bytes=VMEM_LIMIT_BYTES,
        ),
        name="hybrid_block",
    )(*operands)
```

```python
import jax
import jax.numpy as jnp
from jax import lax
from jax.experimental import pallas as pl
from jax.experimental.pallas import tpu as pltpu

D_MODEL = 1024
CHUNK = 64
EPS = 1e-6
N_HEADS_A = 8
HEAD_DIM_A = 64
D_A = N_HEADS_A * HEAD_DIM_A
N_PREV_CHUNKS = 8
REL_CLIP = 128
SGU_CHUNK = 128
N_GROUPS_B = 4
GROUP_DIM_B = 128
D_B = N_GROUPS_B * GROUP_DIM_B
NEG_INF = -1e30

LANES = 128
HEADS_PER_SLAB = LANES // HEAD_DIM_A
N_SLABS = N_HEADS_A // HEADS_PER_SLAB

TILE = 512
Q_GROUP_CHUNKS = 4
Q_GROUP = Q_GROUP_CHUNKS * CHUNK
KEY_WINDOW = (Q_GROUP_CHUNKS + N_PREV_CHUNKS) * CHUNK
PREV = N_PREV_CHUNKS * CHUNK
N_Q_GROUPS = TILE // Q_GROUP
REL_RING = Q_GROUP + KEY_WINDOW

OFF_Q = 0
OFF_K = OFF_Q + D_A
OFF_V = OFF_K + D_A
OFF_GA = OFF_V + D_A
OFF_UB = OFF_GA + D_A
OFF_VB = OFF_UB + D_B
OFF_GB = OFF_VB + D_B
OFF_GATE_A = OFF_GB + D_B
OFF_GATE_B = OFF_GATE_A + D_MODEL
D_IN = OFF_GATE_B + D_MODEL

VMEM_LIMIT_BYTES = 60 * 1024 * 1024
STAGE_ROWS_IN = 64
STAGE_ROWS_SQ = 256

_NT = (((1,), (1,)), ((), ()))


def _proj(h, w_ref, off, width):
    return jnp.dot(h, w_ref[:, off:off + width], preferred_element_type=jnp.float32)


def _load_as_bf16(w_hbm, w_bf16, stage_ref, sem, first_slot):
    rows_per_copy = stage_ref.shape[1]
    n = w_hbm.shape[0] // rows_per_copy
    assert n * rows_per_copy == w_hbm.shape[0]

    def copy(i):
        slot = (first_slot + i) % 2
        return pltpu.make_async_copy(
            w_hbm.at[pl.ds(i * rows_per_copy, rows_per_copy), :], stage_ref.at[slot], sem.at[slot])

    copy(0).start()
    for i in range(n):
        if i + 1 < n:
            copy(i + 1).start()
        copy(i).wait()
        w_bf16[i * rows_per_copy:(i + 1) * rows_per_copy, :] = stage_ref[(first_slot + i) % 2].astype(jnp.bfloat16)
    return (first_slot + n) % 2


def _block_kernel(x_ref, ng_ref, win_hbm, bgate_ref, relb_ref, lng_ref, lnb_ref, ws_ref,
                  bs_ref, wpa_hbm, wpb_hbm, wout_hbm, fg_ref, o_ref,
                  h_ref, q_ref, k_ref, v_ref, ya_ref, bias_ref,
                  win_ref, wpa_ref, wpb_ref, wout_ref, stage_in, stage_sq, sem_in, sem_sq):
    t = pl.program_id(1)
    bf16 = jnp.bfloat16
    f32 = jnp.float32

    @pl.when((pl.program_id(0) == 0) & (t == 0))
    def _():
        _load_as_bf16(win_hbm, win_ref, stage_in, sem_in, 0)
        slot = _load_as_bf16(wpa_hbm, wpa_ref, stage_sq, sem_sq, 0)
        slot = _load_as_bf16(wpb_hbm, wpb_ref, stage_sq, sem_sq, slot)
        _load_as_bf16(wout_hbm, wout_ref, stage_sq, sem_sq, slot)
        r = lax.broadcasted_iota(jnp.int32, (Q_GROUP, KEY_WINDOW), 0) // CHUNK
        c = lax.broadcasted_iota(jnp.int32, (Q_GROUP, KEY_WINDOW), 1) // CHUNK
        band = (c >= r) & (c <= r + N_PREV_CHUNKS)
        for hd in range(N_HEADS_A):
            base = jnp.broadcast_to(relb_ref[hd:hd + 1, :], (Q_GROUP, REL_RING))
            toeplitz = pltpu.roll(base, 0, 1, stride=1, stride_axis=0)
            bias_ref[hd] = jnp.where(band, toeplitz[:, :KEY_WINDOW], NEG_INF)

    @pl.when(t == 0)
    def _():
        k_ref[:PREV, :] = jnp.zeros((PREV, D_A), bf16)
        v_ref[:PREV, :] = jnp.zeros((PREV, D_A), bf16)

    x = x_ref[0]
    inv = lax.rsqrt(jnp.mean(x * x, axis=-1, keepdims=True) + EPS)
    h_ref[...] = (x * inv * ng_ref[...]).astype(bf16)
    h = h_ref[...]

    q_ref[...] = (_proj(h, win_ref, OFF_Q, D_A) * (HEAD_DIM_A ** -0.5)).astype(bf16)
    k_ref[PREV:, :] = _proj(h, win_ref, OFF_K, D_A).astype(bf16)
    v_ref[PREV:, :] = _proj(h, win_ref, OFF_V, D_A).astype(bf16)
    gate_path_a = jax.nn.silu(_proj(h, win_ref, OFF_GA, D_A))

    lane = lax.broadcasted_iota(jnp.int32, (Q_GROUP, LANES), 1)
    first_head = lane < HEAD_DIM_A
    col = lax.broadcasted_iota(jnp.int32, (1, KEY_WINDOW), 1)
    for g in range(N_Q_GROUPS):
        r0 = g * Q_GROUP
        pen = jnp.where((t == 0) & (col < PREV - r0), NEG_INF, 0.0).astype(f32)
        for s in range(N_SLABS):
            c0 = s * LANES
            q2 = q_ref[r0:r0 + Q_GROUP, c0:c0 + LANES]
            zero = jnp.zeros_like(q2)
            qm = jnp.concatenate([jnp.where(first_head, q2, zero),
                                  jnp.where(first_head, zero, q2)], axis=0)
            ks = k_ref[r0:r0 + KEY_WINDOW, c0:c0 + LANES]
            vs = v_ref[r0:r0 + KEY_WINDOW, c0:c0 + LANES]
            sc = lax.dot_general(qm, ks, _NT, preferred_element_type=f32)
            bias = bias_ref[HEADS_PER_SLAB * s:HEADS_PER_SLAB * (s + 1)]
            sc = sc + bias.reshape(HEADS_PER_SLAB * Q_GROUP, KEY_WINDOW) + pen
            m = jnp.max(sc, axis=-1, keepdims=True)
            e = jnp.exp(sc - m)
            l = jnp.sum(e, axis=-1, keepdims=True)
            o = jnp.dot(e.astype(bf16), vs, preferred_element_type=f32) / l
            o2 = jnp.where(first_head, o[:Q_GROUP], o[Q_GROUP:])
            ya_ref[r0:r0 + Q_GROUP, c0:c0 + LANES] = (
                o2 * gate_path_a[r0:r0 + Q_GROUP, c0:c0 + LANES]).astype(bf16)

    k_ref[:PREV, :] = k_ref[PREV:, :]
    v_ref[:PREV, :] = v_ref[PREV:, :]

    p_a = jnp.dot(ya_ref[...], wpa_ref[...], preferred_element_type=f32)

    u = jax.nn.gelu(_proj(h, win_ref, OFF_UB, D_B))
    vb = jax.nn.gelu(_proj(h, win_ref, OFF_VB, D_B))
    mu = jnp.mean(vb, axis=-1, keepdims=True)
    var = jnp.mean(jnp.square(vb - mu), axis=-1, keepdims=True)
    vn = ((vb - mu) * lax.rsqrt(var + EPS) * lng_ref[...] + lnb_ref[...]).astype(bf16)
    gate_path_b = jax.nn.silu(_proj(h, win_ref, OFF_GB, D_B))
    row = lax.broadcasted_iota(jnp.int32, (SGU_CHUNK, SGU_CHUNK), 0)
    colq = lax.broadcasted_iota(jnp.int32, (SGU_CHUNK, SGU_CHUNK), 1)
    causal = row >= colq
    mixed_cols = []
    for gi in range(N_GROUPS_B):
        w_g = jnp.where(causal, ws_ref[gi], 0.0).astype(bf16)
        c0 = gi * GROUP_DIM_B
        rows = []
        for n in range(TILE // SGU_CHUNK):
            r0 = n * SGU_CHUNK
            rows.append(jnp.dot(w_g, vn[r0:r0 + SGU_CHUNK, c0:c0 + GROUP_DIM_B],
                                preferred_element_type=f32) + bs_ref[gi])
        mixed_cols.append(jnp.concatenate(rows, axis=0))
    mixed = jnp.concatenate(mixed_cols, axis=1)
    y_b = (u * mixed * gate_path_b).astype(bf16)
    p_b = jnp.dot(y_b, wpb_ref[...], preferred_element_type=f32)

    ga = jax.nn.sigmoid(_proj(h, win_ref, OFF_GATE_A, D_MODEL) + bgate_ref[:, :D_MODEL])
    gb = jax.nn.sigmoid(_proj(h, win_ref, OFF_GATE_B, D_MODEL) + bgate_ref[:, D_MODEL:])
    merged = (ga * p_a + gb * p_b).astype(bf16)
    y = x_ref[0] + jnp.dot(merged, wout_ref[...], preferred_element_type=f32)
    inv_y = lax.rsqrt(jnp.mean(y * y, axis=-1, keepdims=True) + EPS)
    o_ref[0] = y * inv_y * fg_ref[...]


def _rel_ring(rel_bias):
    heads = rel_bias.shape[0]
    far = lambda n: jnp.broadcast_to(rel_bias[:, -1:], (heads, n))
    near = lambda n: jnp.broadcast_to(rel_bias[:, :1], (heads, n))
    n_far = PREV - REL_CLIP
    n_near = KEY_WINDOW - n_far - rel_bias.shape[1]
    ring = jnp.concatenate([far(n_far), rel_bias[:, ::-1], near(n_near), far(REL_RING - KEY_WINDOW)], axis=1)
    return ring.astype(jnp.float32)


def _const_spec(shape):
    zeros = (0,) * len(shape)
    return pl.BlockSpec(shape, lambda b, t: zeros, pipeline_mode=pl.Buffered(1))


@jax.jit
def kernel(x, norm_g, w_in, b_gate, rel_bias, sgu_ln_g, sgu_ln_b, w_s, b_s, w_pa, w_pb, w_out, final_g):
    batch, seq, d = x.shape
    assert d == D_MODEL and seq % TILE == 0 and norm_g.shape[0] == 1
    bf16 = jnp.bfloat16
    f32 = jnp.float32
    row = lambda a: a.reshape(1, -1).astype(f32)
    bs_b = jnp.broadcast_to(b_s[0][:, :, None], (N_GROUPS_B, SGU_CHUNK, GROUP_DIM_B)).astype(f32)
    hbm = pl.BlockSpec(memory_space=pl.ANY)
    operands_and_specs = (
        (x, pl.BlockSpec((1, TILE, D_MODEL), lambda b, t: (b, t, 0))),
        (row(norm_g[0]), None), (w_in[0], hbm), (row(b_gate[0]), None), (_rel_ring(rel_bias[0]), None),
        (row(sgu_ln_g[0]), None), (row(sgu_ln_b[0]), None), (w_s[0], None), (bs_b, None),
        (w_pa[0], hbm), (w_pb[0], hbm), (w_out[0], hbm), (row(final_g), None),
    )
    operands = [a for a, _ in operands_and_specs]
    in_specs = [_const_spec(a.shape) if spec is None else spec for a, spec in operands_and_specs]
    return pl.pallas_call(
        _block_kernel,
        grid=(batch, seq // TILE),
        in_specs=in_specs,
        out_specs=pl.BlockSpec((1, TILE, D_MODEL), lambda b, t: (b, t, 0)),
        out_shape=jax.ShapeDtypeStruct(x.shape, x.dtype),
        scratch_shapes=[
            pltpu.VMEM((TILE, D_MODEL), bf16),
            pltpu.VMEM((TILE, D_A), bf16),
            pltpu.VMEM((PREV + TILE, D_A), bf16),
            pltpu.VMEM((PREV + TILE, D_A), bf16),
            pltpu.VMEM((TILE, D_A), bf16),
            pltpu.VMEM((N_HEADS_A, Q_GROUP, KEY_WINDOW), f32),
            pltpu.VMEM((D_MODEL, D_IN), bf16),
            pltpu.VMEM((D_A, D_MODEL), bf16),
            pltpu.VMEM((D_B, D_MODEL), bf16),
            pltpu.VMEM((D_MODEL, D_MODEL), bf16),
            pltpu.VMEM((2, STAGE_ROWS_IN, D_IN), f32),
            pltpu.VMEM((2, STAGE_ROWS_SQ, D_MODEL), f32),
            pltpu.SemaphoreType.DMA((2,)),
            pltpu.SemaphoreType.DMA((2,)),
        ],
        compiler_params=pltpu.CompilerParams(
            dimension_semantics=("arbitrary", "arbitrary"),
            vmem_limit_bytes=VMEM_LIMIT_BYTES,
        ),
        name="hybrid_block",
    )(*operands)
```

```python
import jax
import jax.numpy as jnp
from jax import lax
from jax.experimental import pallas as pl
from jax.experimental.pallas import tpu as pltpu

D_MODEL = 1024
CHUNK = 64
EPS = 1e-6
N_HEADS_A = 8
HEAD_DIM_A = 64
D_A = N_HEADS_A * HEAD_DIM_A
N_PREV_CHUNKS = 8
REL_CLIP = 128
SGU_CHUNK = 128
N_GROUPS_B = 4
GROUP_DIM_B = 128
D_B = N_GROUPS_B * GROUP_DIM_B
NEG_INF = -1e30

LANES = 128
HEADS_PER_SLAB = LANES // HEAD_DIM_A
N_SLABS = N_HEADS_A // HEADS_PER_SLAB

TILE = 512
Q_GROUP_CHUNKS = 4
Q_GROUP = Q_GROUP_CHUNKS * CHUNK
KEY_WINDOW = (Q_GROUP_CHUNKS + N_PREV_CHUNKS) * CHUNK
PREV = N_PREV_CHUNKS * CHUNK
N_Q_GROUPS = TILE // Q_GROUP
REL_RING = Q_GROUP + KEY_WINDOW

OFF_Q = 0
OFF_K = OFF_Q + D_A
OFF_V = OFF_K + D_A
OFF_GA = OFF_V + D_A
OFF_UB = OFF_GA + D_A
OFF_VB = OFF_UB + D_B
OFF_GB = OFF_VB + D_B
OFF_GATE_A = OFF_GB + D_B
OFF_GATE_B = OFF_GATE_A + D_MODEL
D_IN = OFF_GATE_B + D_MODEL

VMEM_LIMIT_BYTES = 60 * 1024 * 1024
STAGE_SLOTS = 4
STAGE_ROWS_IN = 64
STAGE_ROWS_SQ = 256

_NT = (((1,), (1,)), ((), ()))


def _proj(h, w_ref, off, width):
    return jnp.dot(h, w_ref[:, off:off + width], preferred_element_type=jnp.float32)


def _load_as_bf16(pairs, stage_ref, sem):
    n_slots, rows_per_copy = stage_ref.shape[0], stage_ref.shape[1]
    chunks = []
    for w_hbm, w_bf16 in pairs:
        assert w_hbm.shape[0] % rows_per_copy == 0 and w_hbm.shape == w_bf16.shape
        chunks += [(w_hbm, w_bf16, r0) for r0 in range(0, w_hbm.shape[0], rows_per_copy)]
    ahead = n_slots - 1

    def copy(i):
        w_hbm, _, r0 = chunks[i]
        slot = i % n_slots
        return pltpu.make_async_copy(w_hbm.at[pl.ds(r0, rows_per_copy), :], stage_ref.at[slot], sem.at[slot])

    for i in range(min(ahead, len(chunks))):
        copy(i).start()
    for i, (_, w_bf16, r0) in enumerate(chunks):
        if i + ahead < len(chunks):
            copy(i + ahead).start()
        copy(i).wait()
        w_bf16[r0:r0 + rows_per_copy, :] = stage_ref[i % n_slots].astype(jnp.bfloat16)


def _block_kernel(x_ref, ng_ref, win_hbm, bgate_ref, relb_ref, lng_ref, lnb_ref, ws_ref,
                  bs_ref, wpa_hbm, wpb_hbm, wout_hbm, fg_ref, o_ref,
                  h_ref, q_ref, k_ref, v_ref, ya_ref, bias_ref,
                  win_ref, wpa_ref, wpb_ref, wout_ref, stage_in, stage_sq, sem_in, sem_sq):
    t = pl.program_id(1)
    bf16 = jnp.bfloat16
    f32 = jnp.float32

    @pl.when((pl.program_id(0) == 0) & (t == 0))
    def _():
        _load_as_bf16([(win_hbm, win_ref)], stage_in, sem_in)
        _load_as_bf16([(wpa_hbm, wpa_ref), (wpb_hbm, wpb_ref), (wout_hbm, wout_ref)], stage_sq, sem_sq)
        r = lax.broadcasted_iota(jnp.int32, (Q_GROUP, KEY_WINDOW), 0) // CHUNK
        c = lax.broadcasted_iota(jnp.int32, (Q_GROUP, KEY_WINDOW), 1) // CHUNK
        band = (c >= r) & (c <= r + N_PREV_CHUNKS)
        for hd in range(N_HEADS_A):
            base = jnp.broadcast_to(relb_ref[hd:hd + 1, :], (Q_GROUP, REL_RING))
            toeplitz = pltpu.roll(base, 0, 1, stride=1, stride_axis=0)
            bias_ref[hd] = jnp.where(band, toeplitz[:, :KEY_WINDOW], NEG_INF)

    @pl.when(t == 0)
    def _():
        k_ref[:PREV, :] = jnp.zeros((PREV, D_A), bf16)
        v_ref[:PREV, :] = jnp.zeros((PREV, D_A), bf16)

    x = x_ref[0]
    inv = lax.rsqrt(jnp.mean(x * x, axis=-1, keepdims=True) + EPS)
    h_ref[...] = (x * inv * ng_ref[...]).astype(bf16)
    h = h_ref[...]

    q_ref[...] = (_proj(h, win_ref, OFF_Q, D_A) * (HEAD_DIM_A ** -0.5)).astype(bf16)
    k_ref[PREV:, :] = _proj(h, win_ref, OFF_K, D_A).astype(bf16)
    v_ref[PREV:, :] = _proj(h, win_ref, OFF_V, D_A).astype(bf16)
    gate_path_a = jax.nn.silu(_proj(h, win_ref, OFF_GA, D_A))

    lane = lax.broadcasted_iota(jnp.int32, (Q_GROUP, LANES), 1)
    first_head = lane < HEAD_DIM_A
    col = lax.broadcasted_iota(jnp.int32, (1, KEY_WINDOW), 1)
    for g in range(N_Q_GROUPS):
        r0 = g * Q_GROUP
        pen = jnp.where((t == 0) & (col < PREV - r0), NEG_INF, 0.0).astype(f32)
        for s in range(N_SLABS):
            c0 = s * LANES
            q2 = q_ref[r0:r0 + Q_GROUP, c0:c0 + LANES]
            zero = jnp.zeros_like(q2)
            qm = jnp.concatenate([jnp.where(first_head, q2, zero),
                                  jnp.where(first_head, zero, q2)], axis=0)
            ks = k_ref[r0:r0 + KEY_WINDOW, c0:c0 + LANES]
            vs = v_ref[r0:r0 + KEY_WINDOW, c0:c0 + LANES]
            sc = lax.dot_general(qm, ks, _NT, preferred_element_type=f32)
            bias = bias_ref[HEADS_PER_SLAB * s:HEADS_PER_SLAB * (s + 1)]
            sc = sc + bias.reshape(HEADS_PER_SLAB * Q_GROUP, KEY_WINDOW) + pen
            m = jnp.max(sc, axis=-1, keepdims=True)
            e = jnp.exp(sc - m)
            l = jnp.sum(e, axis=-1, keepdims=True)
            o = jnp.dot(e.astype(bf16), vs, preferred_element_type=f32) / l
            o2 = jnp.where(first_head, o[:Q_GROUP], o[Q_GROUP:])
            ya_ref[r0:r0 + Q_GROUP, c0:c0 + LANES] = (
                o2 * gate_path_a[r0:r0 + Q_GROUP, c0:c0 + LANES]).astype(bf16)

    k_ref[:PREV, :] = k_ref[PREV:, :]
    v_ref[:PREV, :] = v_ref[PREV:, :]

    p_a = jnp.dot(ya_ref[...], wpa_ref[...], preferred_element_type=f32)

    u = jax.nn.gelu(_proj(h, win_ref, OFF_UB, D_B))
    vb = jax.nn.gelu(_proj(h, win_ref, OFF_VB, D_B))
    mu = jnp.mean(vb, axis=-1, keepdims=True)
    var = jnp.mean(jnp.square(vb - mu), axis=-1, keepdims=True)
    vn = ((vb - mu) * lax.rsqrt(var + EPS) * lng_ref[...] + lnb_ref[...]).astype(bf16)
    gate_path_b = jax.nn.silu(_proj(h, win_ref, OFF_GB, D_B))
    row = lax.broadcasted_iota(jnp.int32, (SGU_CHUNK, SGU_CHUNK), 0)
    colq = lax.broadcasted_iota(jnp.int32, (SGU_CHUNK, SGU_CHUNK), 1)
    causal = row >= colq
    mixed_cols = []
    for gi in range(N_GROUPS_B):
        w_g = jnp.where(causal, ws_ref[gi], 0.0).astype(bf16)
        c0 = gi * GROUP_DIM_B
        rows = []
        for n in range(TILE // SGU_CHUNK):
            r0 = n * SGU_CHUNK
            rows.append(jnp.dot(w_g, vn[r0:r0 + SGU_CHUNK, c0:c0 + GROUP_DIM_B],
                                preferred_element_type=f32) + bs_ref[gi])
        mixed_cols.append(jnp.concatenate(rows, axis=0))
    mixed = jnp.concatenate(mixed_cols, axis=1)
    y_b = (u * mixed * gate_path_b).astype(bf16)
    p_b = jnp.dot(y_b, wpb_ref[...], preferred_element_type=f32)

    ga = jax.nn.sigmoid(_proj(h, win_ref, OFF_GATE_A, D_MODEL) + bgate_ref[:, :D_MODEL])
    gb = jax.nn.sigmoid(_proj(h, win_ref, OFF_GATE_B, D_MODEL) + bgate_ref[:, D_MODEL:])
    merged = (ga * p_a + gb * p_b).astype(bf16)
    y = x_ref[0] + jnp.dot(merged, wout_ref[...], preferred_element_type=f32)
    inv_y = lax.rsqrt(jnp.mean(y * y, axis=-1, keepdims=True) + EPS)
    o_ref[0] = y * inv_y * fg_ref[...]


def _rel_ring(rel_bias):
    heads = rel_bias.shape[0]
    far = lambda n: jnp.broadcast_to(rel_bias[:, -1:], (heads, n))
    near = lambda n: jnp.broadcast_to(rel_bias[:, :1], (heads, n))
    n_far = PREV - REL_CLIP
    n_near = KEY_WINDOW - n_far - rel_bias.shape[1]
    ring = jnp.concatenate([far(n_far), rel_bias[:, ::-1], near(n_near), far(REL_RING - KEY_WINDOW)], axis=1)
    return ring.astype(jnp.float32)


def _const_spec(shape):
    zeros = (0,) * len(shape)
    return pl.BlockSpec(shape, lambda b, t: zeros, pipeline_mode=pl.Buffered(1))


@jax.jit
def kernel(x, norm_g, w_in, b_gate, rel_bias, sgu_ln_g, sgu_ln_b, w_s, b_s, w_pa, w_pb, w_out, final_g):
    batch, seq, d = x.shape
    assert d == D_MODEL and seq % TILE == 0 and norm_g.shape[0] == 1
    bf16 = jnp.bfloat16
    f32 = jnp.float32
    row = lambda a: a.reshape(1, -1).astype(f32)
    bs_b = jnp.broadcast_to(b_s[0][:, :, None], (N_GROUPS_B, SGU_CHUNK, GROUP_DIM_B)).astype(f32)
    hbm = pl.BlockSpec(memory_space=pl.ANY)
    operands_and_specs = (
        (x, pl.BlockSpec((1, TILE, D_MODEL), lambda b, t: (b, t, 0))),
        (row(norm_g[0]), None), (w_in[0], hbm), (row(b_gate[0]), None), (_rel_ring(rel_bias[0]), None),
        (row(sgu_ln_g[0]), None), (row(sgu_ln_b[0]), None), (w_s[0], None), (bs_b, None),
        (w_pa[0], hbm), (w_pb[0], hbm), (w_out[0], hbm), (row(final_g), None),
    )
    operands = [a for a, _ in operands_and_specs]
    in_specs = [_const_spec(a.shape) if spec is None else spec for a, spec in operands_and_specs]
    return pl.pallas_call(
        _block_kernel,
        grid=(batch, seq // TILE),
        in_specs=in_specs,
        out_specs=pl.BlockSpec((1, TILE, D_MODEL), lambda b, t: (b, t, 0)),
        out_shape=jax.ShapeDtypeStruct(x.shape, x.dtype),
        scratch_shapes=[
            pltpu.VMEM((TILE, D_MODEL), bf16),
            pltpu.VMEM((TILE, D_A), bf16),
            pltpu.VMEM((PREV + TILE, D_A), bf16),
            pltpu.VMEM((PREV + TILE, D_A), bf16),
            pltpu.VMEM((TILE, D_A), bf16),
            pltpu.VMEM((N_HEADS_A, Q_GROUP, KEY_WINDOW), f32),
            pltpu.VMEM((D_MODEL, D_IN), bf16),
            pltpu.VMEM((D_A, D_MODEL), bf16),
            pltpu.VMEM((D_B, D_MODEL), bf16),
            pltpu.VMEM((D_MODEL, D_MODEL), bf16),
            pltpu.VMEM((STAGE_SLOTS, STAGE_ROWS_IN, D_IN), f32),
            pltpu.VMEM((STAGE_SLOTS, STAGE_ROWS_SQ, D_MODEL), f32),
            pltpu.SemaphoreType.DMA((STAGE_SLOTS,)),
            pltpu.SemaphoreType.DMA((STAGE_SLOTS,)),
        ],
        compiler_params=pltpu.CompilerParams(
            dimension_semantics=("arbitrary", "arbitrary"),
            vmem_limit_bytes=VMEM_LIMIT_BYTES,
        ),
        name="hybrid_block",
    )(*operands)
```
